```python
import math
import jax
import jax.numpy as jnp
from jax import lax
import numpy as np

D_MODEL = 4096
BATCH = 1
SEQ = 16384
DEPTH = 4

GRID_W = 64
CTX_LEN = 256
N_MOD = 6
EPS = 1e-6

RET_HEADS = 8
RET_DK = 128
RET_DV = 256
RET_QK = RET_HEADS * RET_DK
RET_V = RET_HEADS * RET_DV
RET_CHUNK = 128
ROPE_BASE = 10000.0

S5_WIDTH = 1024
S5_GROUP = 16
S5_GROUPS = S5_WIDTH // S5_GROUP
S5_STATE = 64
S5_DT_MIN = 1e-3
S5_DT_MAX = 1e-1
S5_MAX_RE = -1e-4

HY_WIDTH = 1024
HY_ORDER = 2
HY_EMB = 33
HY_BANDS = (HY_EMB - 1) // 2
HY_HIDDEN = 64
HY_FAST_DECAY = 0.3
HY_SLOW_DECAY = 1.5
HY_TARGET = 1e-2

MIX_WIDTH = RET_V + S5_WIDTH + HY_WIDTH
IN_WIDTH = 2 * RET_QK + 2 * RET_V + S5_WIDTH + (HY_ORDER + 1) * HY_WIDTH
IN_OFFSETS = (RET_QK, 2 * RET_QK, 2 * RET_QK + RET_V, 2 * RET_QK + 2 * RET_V,
              2 * RET_QK + 2 * RET_V + S5_WIDTH)

D_FF_DENSE = 2048
N_EXPERTS = 8
TOP_K = 2
D_FF_EXPERT = 1024
N_DENSE = (DEPTH + 1) // 2
N_MOE = DEPTH // 2

kernel_name = 'hybrid_ret_s5_hyena_moe_dit'


def rmsnorm(x, g):
    xf = x.astype(jnp.float32)
    y = xf * lax.rsqrt(jnp.mean(xf * xf, axis=-1, keepdims=True) + EPS)
    return (y * g.astype(jnp.float32)).astype(x.dtype)


def axial_rope(n_tokens):
    rows = n_tokens // GRID_W
    row = jnp.repeat(jnp.arange(rows, dtype=jnp.float32), GRID_W)
    col = jnp.tile(jnp.arange(GRID_W, dtype=jnp.float32), rows)
    n_freq = RET_DK // 4
    inv = ROPE_BASE ** (-jnp.arange(n_freq, dtype=jnp.float32) / n_freq)
    ang = jnp.concatenate([row[:, None] * inv, col[:, None] * inv], axis=-1)
    return jnp.cos(ang), jnp.sin(ang)


def apply_rope(x, cos, sin):
    half = x.shape[-1] // 2
    x1, x2 = x[..., :half], x[..., half:]
    cos = cos[None, :, None, :]
    sin = sin[None, :, None, :]
    return jnp.concatenate([x1 * cos - x2 * sin, x1 * sin + x2 * cos], axis=-1)


def retention_chunkwise(q, k, v, log_gamma, state0, strict):
    b, L, h, dk = q.shape
    dv = v.shape[-1]
    C = RET_CHUNK
    nc = L // C
    qc = q.reshape(b, nc, C, h, dk)
    kc = k.reshape(b, nc, C, h, dk)
    vc = v.reshape(b, nc, C, h, dv)
    pos = jnp.arange(C, dtype=jnp.float32)
    diff = pos[:, None] - pos[None, :]
    mask = diff > 0 if strict else diff >= 0
    decay = jnp.where(mask[None], jnp.exp(log_gamma[:, None, None] * jnp.where(mask, diff, 0.0)[None]), 0.0)
    scores = jnp.einsum('bcnhd,bcmhd->bchnm', qc, kc) * decay[None, None]
    inner = jnp.einsum('bchnm,bcmhe->bcnhe', scores, vc)
    zeta = jnp.exp(log_gamma[None, :] * (C - 1 - pos)[:, None])
    kv = jnp.einsum('bcmhd,bcmhe->bchde', kc * zeta[None, None, :, :, None], vc)
    chunk_decay = jnp.exp(log_gamma * C)[:, None, None]

    def step(r, kv_i):
        return chunk_decay * r + kv_i, r

    r_final, r_prev = lax.scan(step, state0, jnp.moveaxis(kv, 1, 0))
    xi = jnp.exp(log_gamma[None, :] * (pos + 1.0)[:, None])
    cross = jnp.einsum('bcnhd,cbhde->bcnhe', qc, r_prev) * xi[None, None, :, :, None]
    return (inner + cross).reshape(b, L, h, dv), r_final


def retention_mixer(q_c, k_c, v_c, q_l, k_l, v_l, log_gamma):
    b = q_c.shape[0]
    zeros = jnp.zeros((b, RET_HEADS, RET_DK, RET_DV), jnp.float32)

    def flip(t):
        return jnp.flip(t, axis=1)

    oc_f, r_f = retention_chunkwise(q_c, k_c, v_c, log_gamma[0], zeros, False)
    oc_b, r_b = retention_chunkwise(flip(q_c), flip(k_c), flip(v_c), log_gamma[1], zeros, True)
    ol_f, _ = retention_chunkwise(q_l, k_l, v_l, log_gamma[0], r_f, False)
    ol_b, _ = retention_chunkwise(flip(q_l), flip(k_l), flip(v_l), log_gamma[1], r_b, True)
    return oc_f + flip(oc_b), ol_f + flip(ol_b)


def retention_readout(o, g):
    o = o * lax.rsqrt(jnp.mean(o * o, axis=-1, keepdims=True) + EPS)
    return o.reshape(o.shape[0], o.shape[1], RET_V) * jax.nn.silu(g)


def _linear_combine(left, right):
    a_l, b_l = left
    a_r, b_r = right
    return a_r * a_l, a_r * b_l + b_r


def s5_direction(u, lam_bar, b_bar, x0):
    bu = jnp.einsum('gph,blgh->blgp', b_bar, u)
    bu = bu.at[:, 0].add(lam_bar * x0)
    a = jnp.broadcast_to(lam_bar, bu.shape)
    _, xs = lax.associative_scan(_linear_combine, (a, bu), axis=1)
    return xs


def s5_mixer(u_c, u_l, a_re, a_im, log_dt, b_re, b_im, c_re, c_im, d, w_glu):
    f32 = jnp.float32
    lam = lax.complex(jnp.minimum(a_re.astype(f32), S5_MAX_RE), a_im.astype(f32))
    lam_bar = jnp.exp(lam * jnp.exp(log_dt.astype(f32))[..., None])
    b_mat = lax.complex(b_re.astype(f32), b_im.astype(f32))
    b_bar = ((lam_bar - 1.0) / lam)[..., None] * b_mat[None]
    c_mat = lax.complex(c_re.astype(f32), c_im.astype(f32))

    def groups(u):
        return u.reshape(u.shape[0], u.shape[1], S5_GROUPS, S5_GROUP)

    def flip(t):
        return jnp.flip(t, axis=1)

    g_c, g_l = groups(u_c), groups(u_l)
    x0 = jnp.zeros((u_c.shape[0], S5_GROUPS, S5_STATE), jnp.complex64)
    xc_f = s5_direction(g_c, lam_bar[0], b_bar[0], x0)
    xc_b = s5_direction(flip(g_c), lam_bar[1], b_bar[1], x0)
    xl_f = s5_direction(g_l, lam_bar[0], b_bar[0], xc_f[:, -1])
    xl_b = s5_direction(flip(g_l), lam_bar[1], b_bar[1], xc_b[:, -1])

    def readout(states, u):
        y = jnp.real(jnp.einsum('ghp,blgp->blgh', c_mat, states)).reshape(u.shape) + d.astype(f32) * u
        z = jax.nn.gelu(y) @ w_glu.astype(f32)
        return z[..., :S5_WIDTH] * jax.nn.sigmoid(z[..., S5_WIDTH:])

    return readout(xc_f + flip(xc_b), u_c), readout(xl_f + flip(xl_b), u_l)


def hyena_kernel(n, w1, b1, w2, b2, w3, b3, freq):
    t = jnp.linspace(0.0, 1.0, n, dtype=jnp.float32)[:, None]
    w = 2.0 * math.pi * jnp.arange(n, dtype=jnp.float32)[:, None] / n
    f = jnp.linspace(1e-4, HY_BANDS - 1, HY_BANDS, dtype=jnp.float32)[None, :]
    z = jnp.concatenate([t, jnp.cos(f * w), -jnp.sin(f * w)], axis=-1)
    h = jnp.sin(freq[0] * (z @ w1 + b1))
    h = jnp.sin(freq[1] * (h @ w2 + b2))
    h = (h @ w3 + b3).reshape(n, 2, HY_WIDTH)
    max_decay = math.log(HY_TARGET) / HY_FAST_DECAY
    min_decay = math.log(HY_TARGET) / HY_SLOW_DECAY
    deltas = jnp.abs(jnp.linspace(min_decay, max_decay, HY_WIDTH, dtype=jnp.float32))
    h = h * jnp.exp(-t * deltas)[:, None, :]
    return jnp.concatenate([h[:, 0], jnp.zeros((1, HY_WIDTH), jnp.float32), jnp.flip(h[1:, 1], axis=0)], axis=0)


def hyena_mixer(z, short_w, short_b, filt, bias):
    f32 = jnp.float32
    n = z.shape[1]
    sw = short_w.astype(f32)
    zp = jnp.pad(z, ((0, 0), (1, 1), (0, 0)))
    zc = sw[0] * zp[:, :-2] + sw[1] * zp[:, 1:-1] + sw[2] * zp[:, 2:] + short_b.astype(f32)
    x0, x1, v = jnp.split(zc, HY_ORDER + 1, axis=-1)
    k = hyena_kernel(n, *[p.astype(f32) for p in filt])
    u = x1 * v
    y = jnp.fft.irfft(jnp.fft.rfft(u, n=2 * n, axis=1) * jnp.fft.rfft(k, axis=0)[None], n=2 * n, axis=1)[:, :n]
    return x0 * (y + u * bias.astype(f32))


def token_mixer(h_c, h_l, rope_cos, rope_sin, w_in, w_out, ret_decay,
                s5_a_re, s5_a_im, s5_log_dt, s5_b_re, s5_b_im, s5_c_re, s5_c_im, s5_d, s5_w_glu,
                hy_short_w, hy_short_b, hy_filter, hy_bias, need_ctx):
    f32 = jnp.float32
    q_c, k_c, v_c, g_c, u_c, z_c = jnp.split((h_c @ w_in).astype(f32), IN_OFFSETS, axis=-1)
    q_l, k_l, v_l, g_l, u_l, z_l = jnp.split((h_l @ w_in).astype(f32), IN_OFFSETS, axis=-1)

    def heads(t, dh):
        return t.reshape(t.shape[0], t.shape[1], RET_HEADS, dh)

    scale = RET_DK ** -0.5
    log_gamma = jax.nn.log_sigmoid(ret_decay.astype(f32))
    r_c, r_l = retention_mixer(
        heads(q_c, RET_DK), heads(k_c, RET_DK) * scale, heads(v_c, RET_DV),
        apply_rope(heads(q_l, RET_DK), rope_cos, rope_sin),
        apply_rope(heads(k_l, RET_DK), rope_cos, rope_sin) * scale,
        heads(v_l, RET_DV), log_gamma)
    s_c, s_l = s5_mixer(u_c, u_l, s5_a_re, s5_a_im, s5_log_dt, s5_b_re, s5_b_im,
                        s5_c_re, s5_c_im, s5_d, s5_w_glu)
    y_l = jnp.concatenate([retention_readout(r_l, g_l), s_l,
                           hyena_mixer(z_l, hy_short_w, hy_short_b, hy_filter, hy_bias)], axis=-1)
    out_l = y_l.astype(h_l.dtype) @ w_out
    if not need_ctx:
        return None, out_l
    y_c = jnp.concatenate([retention_readout(r_c, g_c), s_c,
                           hyena_mixer(z_c, hy_short_w, hy_short_b, hy_filter, hy_bias)], axis=-1)
    return y_c.astype(h_c.dtype) @ w_out, out_l


def swiglu(h, w_gate, w_up, w_down):
    return (jax.nn.silu(h @ w_gate) * (h @ w_up)) @ w_down


def moe_swiglu(h, w_router, b_router, w_gate, w_up, w_down):
    logits = (h @ w_router).astype(jnp.float32) + b_router.astype(jnp.float32)
    top_v, top_i = lax.top_k(logits, TOP_K)
    top_w = jax.nn.softmax(top_v, axis=-1)
    combine = jnp.sum(jax.nn.one_hot(top_i, N_EXPERTS, dtype=jnp.float32) * top_w[..., None], axis=-2)
    act = jax.nn.silu(jnp.einsum('bld,edf->blef', h, w_gate)) * jnp.einsum('bld,edf->blef', h, w_up)
    return jnp.einsum('blef,efd->bld', act * combine.astype(act.dtype)[..., None], w_down)


def setup_inputs(seed: int = 0) -> dict:
    key = jax.random.key(seed)
    ks = iter(jax.random.split(key, 48))
    f32 = jnp.float32
    D = D_MODEL

    def nrm(shape, std):
        return std * jax.random.normal(next(ks), shape, f32)

    inp = {}
    inp['x'] = nrm((BATCH, SEQ, D), 1.0)
    inp['c'] = nrm((BATCH, D), 1.0)
    inp['ctx'] = nrm((BATCH, CTX_LEN, D), 1.0)
    inp['c_ctx'] = nrm((D,), 1.0)
    inp['ada_w'] = nrm((DEPTH, D, N_MOD * D), 0.1 * D ** -0.5)
    inp['ada_b'] = nrm((DEPTH, N_MOD * D), 0.02)
    inp['norm_g'] = 1.0 + nrm((DEPTH, 4, D), 0.05)
    inp['w_in'] = nrm((DEPTH, D, IN_WIDTH), D ** -0.5)
    inp['w_out'] = nrm((DEPTH, MIX_WIDTH, D), MIX_WIDTH ** -0.5)
    h_idx = jnp.arange(RET_HEADS, dtype=f32)
    gamma = 1.0 - 2.0 ** (-5.0 - h_idx)
    gamma_logit = jnp.log(gamma) + (5.0 + h_idx) * math.log(2.0)
    inp['ret_decay'] = gamma_logit[None, None, :] + nrm((DEPTH, 2, RET_HEADS), 0.1)
    inp['s5_a_re'] = -0.5 + nrm((DEPTH, 2, S5_GROUPS, S5_STATE), 0.01)
    inp['s5_a_im'] = math.pi * jnp.arange(S5_STATE, dtype=f32) + nrm((DEPTH, 2, S5_GROUPS, S5_STATE), 0.01)
    inp['s5_log_dt'] = jax.random.uniform(next(ks), (DEPTH, 2, S5_GROUPS), f32,
                                          minval=math.log(S5_DT_MIN), maxval=math.log(S5_DT_MAX))
    inp['s5_b_re'] = nrm((DEPTH, S5_GROUPS, S5_STATE, S5_GROUP), (2.0 * S5_GROUP) ** -0.5)
    inp['s5_b_im'] = nrm((DEPTH, S5_GROUPS, S5_STATE, S5_GROUP), (2.0 * S5_GROUP) ** -0.5)
    inp['s5_c_re'] = nrm((DEPTH, S5_GROUPS, S5_GROUP, S5_STATE), (2.0 * S5_STATE) ** -0.5)
    inp['s5_c_im'] = nrm((DEPTH, S5_GROUPS, S5_GROUP, S5_STATE), (2.0 * S5_STATE) ** -0.5)
    inp['s5_d'] = nrm((DEPTH, S5_WIDTH), 1.0)
    inp['s5_w_glu'] = nrm((DEPTH, S5_WIDTH, 2 * S5_WIDTH), S5_WIDTH ** -0.5)
    inp['hy_short_w'] = nrm((DEPTH, 3, (HY_ORDER + 1) * HY_WIDTH), 3.0 ** -0.5)
    inp['hy_short_b'] = nrm((DEPTH, (HY_ORDER + 1) * HY_WIDTH), 0.02)
    inp['hy_w1'] = nrm((DEPTH, HY_EMB, HY_HIDDEN), HY_EMB ** -0.5)
    inp['hy_b1'] = nrm((DEPTH, HY_HIDDEN), 0.02)
    inp['hy_w2'] = nrm((DEPTH, HY_HIDDEN, HY_HIDDEN), HY_HIDDEN ** -0.5)
    inp['hy_b2'] = nrm((DEPTH, HY_HIDDEN), 0.02)
    inp['hy_w3'] = nrm((DEPTH, HY_HIDDEN, 2 * HY_WIDTH), 0.02)
    inp['hy_b3'] = nrm((DEPTH, 2 * HY_WIDTH), 0.002)
    inp['hy_freq'] = 1.0 + nrm((DEPTH, 2, HY_HIDDEN), 0.05)
    inp['hy_bias'] = nrm((DEPTH, HY_WIDTH), 1.0)
    inp['ffn_w_gate'] = nrm((N_DENSE, D, D_FF_DENSE), D ** -0.5)
    inp['ffn_w_up'] = nrm((N_DENSE, D, D_FF_DENSE), D ** -0.5)
    inp['ffn_w_down'] = nrm((N_DENSE, D_FF_DENSE, D), D_FF_DENSE ** -0.5)
    inp['moe_w_router'] = nrm((N_MOE, D, N_EXPERTS), D ** -0.5)
    inp['moe_b_router'] = nrm((N_MOE, N_EXPERTS), 0.01)
    inp['moe_w_gate'] = nrm((N_MOE, N_EXPERTS, D, D_FF_EXPERT), D ** -0.5)
    inp['moe_w_up'] = nrm((N_MOE, N_EXPERTS, D, D_FF_EXPERT), D ** -0.5)
    inp['moe_w_down'] = nrm((N_MOE, N_EXPERTS, D_FF_EXPERT, D), D_FF_EXPERT ** -0.5)
    return inp


def reference(x, c, ctx, c_ctx, ada_w, ada_b, norm_g, w_in, w_out, ret_decay,
              s5_a_re, s5_a_im, s5_log_dt, s5_b_re, s5_b_im, s5_c_re, s5_c_im, s5_d, s5_w_glu,
              hy_short_w, hy_short_b, hy_w1, hy_b1, hy_w2, hy_b2, hy_w3, hy_b3, hy_freq, hy_bias,
              ffn_w_gate, ffn_w_up, ffn_w_down,
              moe_w_router, moe_b_router, moe_w_gate, moe_w_up, moe_w_down):
    rope_cos, rope_sin = axial_rope(x.shape[1])
    cond_l = jax.nn.silu(c)
    cond_c = jax.nn.silu(c_ctx)[None]
    for l in range(DEPTH):
        last = l == DEPTH - 1
        mod_l = jnp.split((cond_l @ ada_w[l] + ada_b[l])[:, None, :], N_MOD, axis=-1)
        mod_c = jnp.split((cond_c @ ada_w[l] + ada_b[l])[:, None, :], N_MOD, axis=-1)

        h_l = rmsnorm(x, norm_g[l, 0]) * (1.0 + mod_l[1]) + mod_l[0]
        h_c = rmsnorm(ctx, norm_g[l, 0]) * (1.0 + mod_c[1]) + mod_c[0]
        hy_filter = (hy_w1[l], hy_b1[l], hy_w2[l], hy_b2[l], hy_w3[l], hy_b3[l], hy_freq[l])
        y_c, y_l = token_mixer(h_c, h_l, rope_cos, rope_sin, w_in[l], w_out[l], ret_decay[l],
                               s5_a_re[l], s5_a_im[l], s5_log_dt[l], s5_b_re[l], s5_b_im[l],
                               s5_c_re[l], s5_c_im[l], s5_d[l], s5_w_glu[l],
                               hy_short_w[l], hy_short_b[l], hy_filter, hy_bias[l], not last)
        x = x + mod_l[2] * rmsnorm(y_l, norm_g[l, 1])
        if not last:
            ctx = ctx + mod_c[2] * rmsnorm(y_c, norm_g[l, 1])

        j = l // 2
        h_l = rmsnorm(x, norm_g[l, 2]) * (1.0 + mod_l[4]) + mod_l[3]
        if l % 2 == 0:
            f_l = swiglu(h_l, ffn_w_gate[j], ffn_w_up[j], ffn_w_down[j])
        else:
            f_l = moe_swiglu(h_l, moe_w_router[j], moe_b_router[j], moe_w_gate[j], moe_w_up[j], moe_w_down[j])
        x = x + mod_l[5] * rmsnorm(f_l, norm_g[l, 3])
        if not last:
            h_c = rmsnorm(ctx, norm_g[l, 2]) * (1.0 + mod_c[4]) + mod_c[3]
            if l % 2 == 0:
                f_c = swiglu(h_c, ffn_w_gate[j], ffn_w_up[j], ffn_w_down[j])
            else:
                f_c = moe_swiglu(h_c, moe_w_router[j], moe_b_router[j], moe_w_gate[j], moe_w_up[j], moe_w_down[j])
            ctx = ctx + mod_c[5] * rmsnorm(f_c, norm_g[l, 3])
    return x
```

```python
import functools
import math

import jax
import jax.numpy as jnp
import numpy as np
from jax import lax
from jax.experimental import pallas as pl
from jax.experimental.pallas import tpu as pltpu

D_MODEL = 4096
GRID_W = 64
N_MOD = 6
EPS = 1e-6

RET_HEADS = 8
RET_DK = 128
RET_DV = 256
RET_QK = RET_HEADS * RET_DK
RET_V = RET_HEADS * RET_DV
RET_CHUNK = 128
ROPE_BASE = 10000.0

S5_WIDTH = 1024
S5_GROUP = 16
S5_GROUPS = S5_WIDTH // S5_GROUP
S5_STATE = 64
S5_MAX_RE = -1e-4
S5_LANES = S5_GROUPS * S5_STATE
S5_BLK_GROUPS = 8
S5_BLK = S5_BLK_GROUPS * S5_STATE
S5_NBLK = S5_GROUPS // S5_BLK_GROUPS

HY_WIDTH = 1024
HY_ORDER = 2
HY_EMB = 33
HY_EMB_PAD = 64
HY_BANDS = (HY_EMB - 1) // 2
HY_FAST_DECAY = 0.3
HY_SLOW_DECAY = 1.5
HY_TARGET = 1e-2
HY_N2 = 256
HY_MIN_N1 = 16

COL_Q, COL_K, COL_V, COL_G = 0, RET_QK, 2 * RET_QK, 2 * RET_QK + RET_V
COL_U = 2 * RET_QK + 2 * RET_V
COL_Z = COL_U + S5_WIDTH

N_EXPERTS = 8
ROUTER_PAD = 128
NEG_BIG = -1e30

BF16 = jnp.bfloat16
F32 = jnp.float32
HIGHEST = lax.Precision.HIGHEST

VMEM_LIMIT_BYTES = 56 * 1024 * 1024
CAST_ROWS = 256
SUBLANES = 8
LANES = 128


def _params(*sem):
    return pltpu.CompilerParams(dimension_semantics=sem, vmem_limit_bytes=VMEM_LIMIT_BYTES)


def _cast_weight(w_ref, wb_ref):
    k = w_ref.shape[0]
    rows = min(CAST_ROWS, k)

    def body(c, carry):
        r = pl.multiple_of(c * rows, rows)
        wb_ref[pl.ds(r, rows), :] = w_ref[pl.ds(r, rows), :].astype(BF16)
        return carry

    lax.fori_loop(0, k // rows, body, 0)


def _mm_kernel(a_ref, w_ref, o_ref, wb_ref):
    @pl.when(pl.program_id(1) == 0)
    def _():
        _cast_weight(w_ref, wb_ref)

    o_ref[...] = jnp.dot(a_ref[...], wb_ref[...], preferred_element_type=F32).astype(o_ref.dtype)


def matmul(a, w, *, tm, tn, out_dtype=F32):
    m, k = a.shape
    k2, n = w.shape
    assert k == k2 and m % tm == 0 and n % tn == 0, (a.shape, w.shape, tm, tn)
    return pl.pallas_call(
        _mm_kernel,
        grid=(n // tn, m // tm),
        in_specs=[pl.BlockSpec((tm, k), lambda j, i: (i, 0)),
                  pl.BlockSpec((k, tn), lambda j, i: (0, j))],
        out_specs=pl.BlockSpec((tm, tn), lambda j, i: (i, j)),
        out_shape=jax.ShapeDtypeStruct((m, n), out_dtype),
        scratch_shapes=[pltpu.VMEM((k, tn), BF16)],
        compiler_params=_params("arbitrary", "arbitrary"),
    )(a, w)


def _glu_kernel(a_ref, wg_ref, wu_ref, c_ref, o_ref, wgb_ref, wub_ref):
    @pl.when(pl.program_id(2) == 0)
    def _():
        _cast_weight(wg_ref, wgb_ref)
        _cast_weight(wu_ref, wub_ref)

    a = a_ref[...]
    g = jnp.dot(a, wgb_ref[...], preferred_element_type=F32)
    u = jnp.dot(a, wub_ref[...], preferred_element_type=F32)
    act = g * jax.nn.sigmoid(g) * u
    e = pl.program_id(0)
    c = c_ref[...]
    lane = lax.broadcasted_iota(jnp.int32, c.shape, 1)
    ce = jnp.sum(jnp.where(lane == e, c, 0.0), axis=1, keepdims=True)
    o_ref[...] = (act * ce).astype(o_ref.dtype)


def expert_glu(a, w_gate, w_up, combine, *, tm, tn):
    m, k = a.shape
    e, k2, f = w_gate.shape
    assert k == k2 and m % tm == 0 and f % tn == 0
    nj = f // tn
    return pl.pallas_call(
        _glu_kernel,
        grid=(e, nj, m // tm),
        in_specs=[pl.BlockSpec((tm, k), lambda e_, j, i: (i, 0)),
                  pl.BlockSpec((None, k, tn), lambda e_, j, i: (e_, 0, j)),
                  pl.BlockSpec((None, k, tn), lambda e_, j, i: (e_, 0, j)),
                  pl.BlockSpec((tm, combine.shape[1]), lambda e_, j, i: (i, 0))],
        out_specs=pl.BlockSpec((tm, tn), lambda e_, j, i: (i, e_ * nj + j)),
        out_shape=jax.ShapeDtypeStruct((m, e * f), BF16),
        scratch_shapes=[pltpu.VMEM((k, tn), BF16), pltpu.VMEM((k, tn), BF16)],
        compiler_params=_params("arbitrary", "arbitrary", "arbitrary"),
    )(a, w_gate, w_up, combine)


def _ada_kernel(a_ref, w_ref, b_ref, o_ref):
    o_ref[...] = jnp.dot(a_ref[...], w_ref[...].astype(BF16), preferred_element_type=F32) + b_ref[...]


def ada_modulation(cond, ada_w, ada_b, *, tn=1024):
    depth, d, n = ada_w.shape
    return pl.pallas_call(
        _ada_kernel,
        grid=(depth, n // tn),
        in_specs=[pl.BlockSpec((SUBLANES, d), lambda l, j: (0, 0)),
                  pl.BlockSpec((None, d, tn), lambda l, j: (l, 0, j)),
                  pl.BlockSpec((None, 1, tn), lambda l, j: (l, 0, j))],
        out_specs=pl.BlockSpec((None, SUBLANES, tn), lambda l, j: (l, 0, j)),
        out_shape=jax.ShapeDtypeStruct((depth, SUBLANES, n), F32),
        compiler_params=_params("arbitrary", "arbitrary"),
    )(cond, ada_w, ada_b)


NORM_ROWS = 256


def _rms(x):
    return x * lax.rsqrt(jnp.mean(x * x, axis=-1, keepdims=True) + EPS)


def _modulated(x, g, mod_ref, row, shift_idx):
    shift = mod_ref[shift_idx, row:row + 1, :]
    scale = mod_ref[shift_idx + 1, row:row + 1, :]
    return _rms(x) * g * (1.0 + scale) + shift


def _prenorm_kernel(x_ref, g_ref, mod_ref, h_ref, *, row):
    h_ref[...] = _modulated(x_ref[...], g_ref[0:1, :], mod_ref, row, 0).astype(BF16)


def prenorm(x, g4, mod_l, row):
    m, d = x.shape
    tr = min(NORM_ROWS, m)
    return pl.pallas_call(
        functools.partial(_prenorm_kernel, row=row),
        grid=(m // tr,),
        in_specs=[pl.BlockSpec((tr, d), lambda i: (i, 0)),
                  pl.BlockSpec((4, d), lambda i: (0, 0)),
                  pl.BlockSpec((N_MOD, SUBLANES, d), lambda i: (0, 0, 0))],
        out_specs=pl.BlockSpec((tr, d), lambda i: (i, 0)),
        out_shape=jax.ShapeDtypeStruct((m, d), BF16),
        compiler_params=_params("arbitrary"),
    )(x, g4, mod_l)


def _top2_combine(logits):
    lane = lax.broadcasted_iota(jnp.int32, logits.shape, 1)
    big = jnp.int32(logits.shape[1])
    v1 = jnp.max(logits, axis=1, keepdims=True)
    i1 = jnp.min(jnp.where(logits == v1, lane, big), axis=1, keepdims=True)
    rest = jnp.where(lane == i1, NEG_BIG, logits)
    v2 = jnp.max(rest, axis=1, keepdims=True)
    i2 = jnp.min(jnp.where(rest == v2, lane, big), axis=1, keepdims=True)
    e2 = jnp.exp(v2 - v1)
    w1 = 1.0 / (1.0 + e2)
    w2 = e2 * w1
    return jnp.where(lane == i1, w1, 0.0) + jnp.where(lane == i2, w2, 0.0)


def _resid_kernel(*refs, row, stage, has_next, has_router):
    x_ref, y_ref, g_ref, mod_ref = refs[:4]
    pos = 4
    if has_router:
        wr_ref, br_ref = refs[pos:pos + 2]
        pos += 2
    xo_ref = refs[pos]
    pos += 1
    gate = mod_ref[3 * stage + 2, row:row + 1, :]
    xn = x_ref[...] + gate * (_rms(y_ref[...]) * g_ref[2 * stage + 1:2 * stage + 2, :])
    xo_ref[...] = xn
    if has_next:
        h_ref = refs[pos]
        pos += 1
        h = _modulated(xn, g_ref[2:3, :], mod_ref, row, 3)
        h_ref[...] = h.astype(BF16)
        if has_router:
            c_ref = refs[pos]
            logits = jnp.dot(h, wr_ref[...], preferred_element_type=F32, precision=HIGHEST) + br_ref[...]
            c_ref[...] = _top2_combine(logits)


def resid_update(x, y, g4, mod_l, row, stage, *, has_next, router=None):
    m, d = x.shape
    tr = min(NORM_ROWS, m)
    row_spec = pl.BlockSpec((tr, d), lambda i: (i, 0))
    in_specs = [row_spec, row_spec, pl.BlockSpec((4, d), lambda i: (0, 0)),
                pl.BlockSpec((N_MOD, SUBLANES, d), lambda i: (0, 0, 0))]
    args = [x, y, g4, mod_l]
    out_specs = [row_spec]
    out_shape = [jax.ShapeDtypeStruct((m, d), F32)]
    if router is not None:
        in_specs += [pl.BlockSpec((d, ROUTER_PAD), lambda i: (0, 0)), pl.BlockSpec((1, ROUTER_PAD), lambda i: (0, 0))]
        args += list(router)
    if has_next:
        out_specs.append(row_spec)
        out_shape.append(jax.ShapeDtypeStruct((m, d), BF16))
        if router is not None:
            out_specs.append(pl.BlockSpec((tr, ROUTER_PAD), lambda i: (i, 0)))
            out_shape.append(jax.ShapeDtypeStruct((m, ROUTER_PAD), F32))
    return pl.pallas_call(
        functools.partial(_resid_kernel, row=row, stage=stage, has_next=has_next, has_router=router is not None),
        grid=(m // tr,),
        in_specs=in_specs,
        out_specs=out_specs,
        out_shape=out_shape,
        compiler_params=_params("arbitrary"),
    )(*args)


def _rope(x, cos, sin):
    return x * cos + pltpu.roll(x, RET_DK // 2, axis=1) * sin


def _ret_kernel(lg_ref, q_ref, k_ref, v_ref, g_ref, cos_ref, sin_ref, r0_ref, o_ref, rfin_ref, ob_ref, r_ref,
                *, nsteps, nchunks):
    h = pl.program_id(0)
    p = pl.program_id(1)
    s = pl.program_id(2)
    C = RET_CHUNK
    fwd = p == 1

    @pl.when(s == 0)
    def _():
        r_ref[...] = r0_ref[...]

    lg = lg_ref[1 - p, h]
    n_i = lax.broadcasted_iota(jnp.int32, (C, C), 0)
    m_i = lax.broadcasted_iota(jnp.int32, (C, C), 1)
    dist = jnp.where(fwd, n_i - m_i, m_i - n_i)
    valid = dist >= jnp.where(fwd, 0, 1)
    decay = jnp.where(valid, jnp.exp(lg * jnp.maximum(dist, 0).astype(F32)), 0.0)
    pos = lax.broadcasted_iota(jnp.int32, (C, 1), 0)
    xi = jnp.exp(lg * jnp.where(fwd, pos + 1, C - pos).astype(F32))
    zeta = jnp.exp(lg * jnp.where(fwd, C - 1 - pos, pos).astype(F32))
    chunk_decay = jnp.exp(lg * C)
    scale = RET_DK ** -0.5
    row0 = pl.multiple_of(jnp.where(fwd, s, nsteps - 1 - s) * (nchunks * C), C)

    for j in range(nchunks):
        cj = jnp.where(fwd, j, nchunks - 1 - j)
        r = pl.multiple_of(cj * C, C)
        cos = cos_ref[pl.ds(r, C), :]
        sin = sin_ref[pl.ds(r, C), :]
        q = _rope(q_ref[pl.ds(r, C), :], cos, sin)
        k = _rope(k_ref[pl.ds(r, C), :], cos, sin) * scale
        vb = v_ref[pl.ds(r, C), :].astype(BF16)
        qb = q.astype(BF16)
        scores = lax.dot_general(qb, k.astype(BF16), (((1,), (1,)), ((), ())), preferred_element_type=F32)
        inner = jnp.dot((scores * decay).astype(BF16), vb, preferred_element_type=F32)
        state = r_ref[...]
        cross = jnp.dot(qb, state.astype(BF16), preferred_element_type=F32) * xi
        o = inner + cross
        kv = lax.dot_general((k * zeta).astype(BF16), vb, (((0,), (0,)), ((), ())), preferred_element_type=F32)
        r_ref[...] = chunk_decay * state + kv
        orow = pl.multiple_of(row0 + r, C)

        @pl.when(p == 0)
        def _():
            ob_ref[pl.ds(orow, C), :] = o

        @pl.when(p == 1)
        def _():
            t = o + ob_ref[pl.ds(orow, C), :]
            t = t * lax.rsqrt(jnp.mean(t * t, axis=-1, keepdims=True) + EPS)
            g = g_ref[pl.ds(r, C), :]
            o_ref[pl.ds(r, C), :] = (t * (g * jax.nn.sigmoid(g))).astype(o_ref.dtype)

    @pl.when(s == nsteps - 1)
    def _():
        rfin_ref[...] = r_ref[...]


def retention(z, log_gamma, rope_cos, rope_sin, r0):
    L = z.shape[0]
    nchunks = min(4, L // RET_CHUNK)
    rows = nchunks * RET_CHUNK
    nsteps = L // rows

    def rmap(col0, width):
        base = col0 // width
        return lambda h, p, s: (p * s + (1 - p) * (nsteps - 1 - s), base + h)

    tab_spec = pl.BlockSpec((rows, RET_DK), lambda h, p, s: (p * s + (1 - p) * (nsteps - 1 - s), 0))
    st_spec = pl.BlockSpec((None, None, RET_DK, RET_DV), lambda h, p, s: (1 - p, h, 0, 0))
    return pl.pallas_call(
        functools.partial(_ret_kernel, nsteps=nsteps, nchunks=nchunks),
        grid=(RET_HEADS, 2, nsteps),
        in_specs=[pl.BlockSpec(memory_space=pltpu.SMEM),
                  pl.BlockSpec((rows, RET_DK), rmap(COL_Q, RET_DK)),
                  pl.BlockSpec((rows, RET_DK), rmap(COL_K, RET_DK)),
                  pl.BlockSpec((rows, RET_DV), rmap(COL_V, RET_DV)),
                  pl.BlockSpec((rows, RET_DV), rmap(COL_G, RET_DV)),
                  tab_spec, tab_spec, st_spec],
        out_specs=[pl.BlockSpec((rows, RET_DV), lambda h, p, s: (p * s, h)), st_spec],
        out_shape=[jax.ShapeDtypeStruct((L, RET_V), BF16),
                   jax.ShapeDtypeStruct((2, RET_HEADS, RET_DK, RET_DV), F32)],
        scratch_shapes=[pltpu.VMEM((L, RET_DV), F32), pltpu.VMEM((RET_DK, RET_DV), F32)],
        compiler_params=_params("arbitrary", "arbitrary", "arbitrary"),
    )(log_gamma, z, z, z, z, rope_cos, rope_sin, r0)


def rope_tables(n_tokens):
    rows = n_tokens // GRID_W
    row = jnp.repeat(jnp.arange(rows, dtype=F32), GRID_W)
    col = jnp.tile(jnp.arange(GRID_W, dtype=F32), rows)
    n_freq = RET_DK // 4
    inv = ROPE_BASE ** (-jnp.arange(n_freq, dtype=F32) / n_freq)
    ang = jnp.concatenate([row[:, None] * inv, col[:, None] * inv], axis=-1)
    cos, sin = jnp.cos(ang), jnp.sin(ang)
    return jnp.concatenate([cos, cos], axis=-1), jnp.concatenate([-sin, sin], axis=-1)


S5_ROWS = 256


def _s5_scan_block(x_re, x_im, pw, c_re, c_im, reverse):
    for d, (m_re, m_im) in zip((1, 2, 4), pw["steps"]):
        shift = (SUBLANES - d) if reverse else d
        s_re = pltpu.roll(x_re, shift, axis=0)
        s_im = pltpu.roll(x_im, shift, axis=0)
        x_re, x_im = x_re + (m_re * s_re - m_im * s_im), x_im + (m_re * s_im + m_im * s_re)
    e_re, e_im = pw["edge"]
    x_re = x_re + (e_re * c_re - e_im * c_im)
    x_im = x_im + (e_re * c_im + e_im * c_re)
    return x_re, x_im


def _s5_kernel(*refs, nsteps, reverse):
    if reverse:
        u_ref, bmat_ref, cmat_ref, pow_ref, x0_ref, y_out_ref, xfin_ref, bu_ref, carry_ref = refs
    else:
        (u_ref, bmat_ref, cmat_ref, pow_ref, x0_ref, d_ref, wglu_ref, yb_ref,
         out_ref, xfin_ref, bu_ref, carry_ref, y_ref, wb_ref) = refs
    s = pl.program_id(0)
    rows = u_ref.shape[0]
    nb = rows // SUBLANES
    W = S5_BLK

    @pl.when(s == 0)
    def _():
        carry_ref[...] = x0_ref[...]
        if not reverse:
            _cast_weight(wglu_ref, wb_ref)

    u = u_ref[...]
    ub = u.astype(BF16)
    y_dst = y_out_ref if reverse else y_ref
    for j in range(S5_NBLK):
        bu_ref[...] = jnp.dot(ub[:, j * LANES:(j + 1) * LANES], bmat_ref[j], preferred_element_type=F32)
        lo = j * W
        pw = {
            "steps": [(pow_ref[2 * t, :, lo:lo + W], pow_ref[2 * t + 1, :, lo:lo + W]) for t in range(3)],
            "edge": (pow_ref[6, :, lo:lo + W], pow_ref[7, :, lo:lo + W]),
        }
        c0 = (carry_ref[0:1, lo:lo + W], carry_ref[1:2, lo:lo + W])

        def body(b, c, pw=pw):
            bi = (nb - 1 - b) if reverse else b
            r = pl.multiple_of(bi * SUBLANES, SUBLANES)
            x_re, x_im = _s5_scan_block(bu_ref[pl.ds(r, SUBLANES), 0:W], bu_ref[pl.ds(r, SUBLANES), W:2 * W],
                                        pw, c[0], c[1], reverse)
            bu_ref[pl.ds(r, SUBLANES), 0:W] = x_re
            bu_ref[pl.ds(r, SUBLANES), W:2 * W] = x_im
            e = 0 if reverse else SUBLANES - 1
            return x_re[e:e + 1, :], x_im[e:e + 1, :]

        c_re, c_im = lax.fori_loop(0, nb, body, c0)
        carry_ref[0:1, lo:lo + W] = c_re
        carry_ref[1:2, lo:lo + W] = c_im
        y_dst[:, j * LANES:(j + 1) * LANES] = jnp.dot(bu_ref[...].astype(BF16), cmat_ref[j],
                                                      preferred_element_type=F32)

    if not reverse:
        y = y_ref[...] + yb_ref[...] + d_ref[...] * u
        z = jnp.dot(jax.nn.gelu(y).astype(BF16), wb_ref[...], preferred_element_type=F32)
        out_ref[...] = (z[:, :S5_WIDTH] * jax.nn.sigmoid(z[:, S5_WIDTH:])).astype(out_ref.dtype)

    @pl.when(s == nsteps - 1)
    def _():
        xfin_ref[...] = carry_ref[...]


def s5_tables(a_re, a_im, log_dt, b_re, b_im, c_re, c_im):
    lam = lax.complex(jnp.minimum(a_re.astype(F32), S5_MAX_RE), a_im.astype(F32))
    lam_bar = jnp.exp(lam * jnp.exp(log_dt.astype(F32))[..., None])
    b_mat = lax.complex(b_re.astype(F32), b_im.astype(F32))
    b_bar = ((lam_bar - 1.0) / lam)[..., None] * b_mat[None]
    eye = jnp.eye(S5_BLK_GROUPS, dtype=F32)

    def in_map(m):
        m = m.reshape(S5_NBLK, S5_BLK_GROUPS, S5_STATE, S5_GROUP)
        return jnp.einsum('jgph,gk->jghkp', m, eye).reshape(S5_NBLK, LANES, S5_BLK)

    bmat = jnp.stack([jnp.concatenate([in_map(jnp.real(b_bar[d])), in_map(jnp.imag(b_bar[d]))], axis=-1)
                      for d in range(2)]).astype(BF16)

    def out_map(m):
        m = m.reshape(S5_NBLK, S5_BLK_GROUPS, S5_GROUP, S5_STATE)
        return jnp.einsum('jghp,gk->jgpkh', m, eye).reshape(S5_NBLK, S5_BLK, LANES)

    cmat = jnp.concatenate([out_map(c_re.astype(F32)), out_map(-c_im.astype(F32))], axis=1).astype(BF16)

    lam_flat = lam_bar.reshape(2, 1, S5_LANES)
    r = jnp.arange(SUBLANES)[None, :, None]

    def masked_pow(d, reverse):
        ok = (r + d < SUBLANES) if reverse else (r >= d)
        return jnp.where(ok, lam_flat[1 if reverse else 0] ** d, 0.0)

    def tables(reverse):
        rows = []
        for d in (1, 2, 4):
            m = masked_pow(d, reverse)
            rows += [jnp.real(m), jnp.imag(m)]
        dist = (SUBLANES - r) if reverse else (r + 1)
        e = lam_flat[1 if reverse else 0] ** dist
        rows += [jnp.real(e), jnp.imag(e)]
        return jnp.stack([jnp.broadcast_to(t, (1, SUBLANES, S5_LANES))[0] for t in rows])

    pows = jnp.stack([tables(False), tables(True)]).astype(F32)
    return bmat, cmat, pows


def s5(z, tables, x0, d_skip, w_glu):
    bmat, cmat, pows = tables
    L = z.shape[0]
    rows = min(S5_ROWS, L)
    nsteps = L // rows
    ucol = COL_U // S5_WIDTH

    def call(reverse, extra_in, extra_specs, out_dtype, scratch):
        d = 1 if reverse else 0
        rmap = (lambda s: (nsteps - 1 - s, 0)) if reverse else (lambda s: (s, 0))
        umap = (lambda s: (nsteps - 1 - s, ucol)) if reverse else (lambda s: (s, ucol))
        st_spec = pl.BlockSpec((None, SUBLANES, S5_LANES), lambda s: (d, 0, 0))
        return pl.pallas_call(
            functools.partial(_s5_kernel, nsteps=nsteps, reverse=reverse),
            grid=(nsteps,),
            in_specs=[pl.BlockSpec((rows, S5_WIDTH), umap),
                      pl.BlockSpec((None, S5_NBLK, LANES, 2 * S5_BLK), lambda s: (d, 0, 0, 0)),
                      pl.BlockSpec((S5_NBLK, 2 * S5_BLK, LANES), lambda s: (0, 0, 0)),
                      pl.BlockSpec((None, 8, SUBLANES, S5_LANES), lambda s: (d, 0, 0, 0)),
                      st_spec] + [sp(rmap) for sp in extra_specs],
            out_specs=[pl.BlockSpec((rows, S5_WIDTH), rmap), pl.BlockSpec((SUBLANES, S5_LANES), lambda s: (0, 0))],
            out_shape=[jax.ShapeDtypeStruct((L, S5_WIDTH), out_dtype),
                       jax.ShapeDtypeStruct((SUBLANES, S5_LANES), F32)],
            scratch_shapes=[pltpu.VMEM((rows, 2 * S5_BLK), F32), pltpu.VMEM((SUBLANES, S5_LANES), F32)] + scratch,
            compiler_params=_params("arbitrary"),
        )(z, bmat, cmat, pows, x0, *extra_in)

    yb, xfin_b = call(True, [], [], F32, [])
    out, xfin_f = call(
        False, [d_skip, w_glu, yb],
        [lambda rmap: pl.BlockSpec((1, S5_WIDTH), lambda s: (0, 0)),
         lambda rmap: pl.BlockSpec((S5_WIDTH, 2 * S5_WIDTH), lambda s: (0, 0)),
         lambda rmap: pl.BlockSpec((rows, S5_WIDTH), rmap)],
        BF16,
        [pltpu.VMEM((rows, S5_WIDTH), F32), pltpu.VMEM((S5_WIDTH, 2 * S5_WIDTH), BF16)])
    return out, jnp.stack([xfin_f, xfin_b])


HY_ROWS = 512
HY_COLS = 8192


def _hy_pre_kernel(*refs, nsteps):
    mains, prevs, nexts = refs[0:3], refs[3:6], refs[6:9]
    sws, sbs = refs[9:12], refs[12:15]
    x0_ref, u_ref = refs[15:17]
    i = pl.program_id(0)
    tr = mains[0].shape[0]
    row = lax.broadcasted_iota(jnp.int32, (tr, 1), 0)
    parts = []
    for c in range(HY_ORDER + 1):
        z = mains[c][...]
        before = jnp.where(i == 0, 0.0, prevs[c][SUBLANES - 1:SUBLANES, :])
        after = jnp.where(i == nsteps - 1, 0.0, nexts[c][0:1, :])
        down = jnp.where(row == 0, before, pltpu.roll(z, 1, axis=0))
        up = jnp.where(row == tr - 1, after, pltpu.roll(z, tr - 1, axis=0))
        sw = sws[c]
        parts.append(sw[0:1, :] * down + sw[1:2, :] * z + sw[2:3, :] * up + sbs[c][...])
    x0_ref[...] = parts[0]
    u_ref[...] = parts[1] * parts[2]


def hyena_pre(z, short_w, short_b):
    L = z.shape[0]
    tr = min(HY_ROWS, L)
    nsteps = L // tr
    per = tr // SUBLANES
    last8 = L // SUBLANES - 1
    cb = COL_Z // HY_WIDTH
    specs, args = [], []
    for kind in range(3):
        for c in range(HY_ORDER + 1):
            if kind == 0:
                specs.append(pl.BlockSpec((tr, HY_WIDTH), lambda i, c=c: (i, cb + c)))
            elif kind == 1:
                specs.append(pl.BlockSpec((SUBLANES, HY_WIDTH), lambda i, c=c: (jnp.maximum(i * per - 1, 0), cb + c)))
            else:
                specs.append(pl.BlockSpec((SUBLANES, HY_WIDTH), lambda i, c=c: (jnp.minimum((i + 1) * per, last8), cb + c)))
            args.append(z)
    for c in range(HY_ORDER + 1):
        specs.append(pl.BlockSpec((3, HY_WIDTH), lambda i, c=c: (0, c)))
        args.append(short_w)
    sb = short_b.reshape(1, -1)
    for c in range(HY_ORDER + 1):
        specs.append(pl.BlockSpec((1, HY_WIDTH), lambda i, c=c: (0, c)))
        args.append(sb)
    out_spec = pl.BlockSpec((tr, HY_WIDTH), lambda i: (i, 0))
    return pl.pallas_call(
        functools.partial(_hy_pre_kernel, nsteps=nsteps),
        grid=(nsteps,),
        in_specs=specs,
        out_specs=[out_spec, out_spec],
        out_shape=[jax.ShapeDtypeStruct((L, HY_WIDTH), F32)] * 2,
        compiler_params=_params("arbitrary"),
    )(*args)


def _hy_filter_kernel(z_ref, w1_ref, b1_ref, w2_ref, b2_ref, w3_ref, b3_ref, fr_ref, dl_ref, ff_ref, fb_ref):
    i = pl.program_id(0)
    z = z_ref[...]
    tr = z.shape[0]
    h = jnp.sin(fr_ref[0:1, :] * (jnp.dot(z, w1_ref[...], preferred_element_type=F32, precision=HIGHEST) + b1_ref[...]))
    h = jnp.sin(fr_ref[1:2, :] * (jnp.dot(h, w2_ref[...], preferred_element_type=F32, precision=HIGHEST) + b2_ref[...]))
    h = jnp.dot(h, w3_ref[...], preferred_element_type=F32, precision=HIGHEST) + b3_ref[...]
    decay = jnp.exp(-z[:, 0:1] * dl_ref[...])
    ff_ref[...] = h[:, :HY_WIDTH] * decay
    row = i * tr + lax.broadcasted_iota(jnp.int32, (tr, 1), 0)
    fb_ref[...] = jnp.where(row == 0, 0.0, h[:, HY_WIDTH:] * decay)


def hyena_features(n):
    t = jnp.linspace(0.0, 1.0, n, dtype=F32)[:, None]
    w = 2.0 * math.pi * jnp.arange(n, dtype=F32)[:, None] / n
    f = jnp.linspace(1e-4, HY_BANDS - 1, HY_BANDS, dtype=F32)[None, :]
    z = jnp.concatenate([t, jnp.cos(f * w), -jnp.sin(f * w)], axis=-1)
    return jnp.pad(z, ((0, 0), (0, HY_EMB_PAD - HY_EMB)))


def hyena_filter(n, w1, b1, w2, b2, w3, b3, freq):
    tr = min(HY_ROWS, n)
    max_decay = math.log(HY_TARGET) / HY_FAST_DECAY
    min_decay = math.log(HY_TARGET) / HY_SLOW_DECAY
    deltas = jnp.abs(jnp.linspace(min_decay, max_decay, HY_WIDTH, dtype=F32))[None, :]
    w1p = jnp.pad(w1.astype(F32), ((0, HY_EMB_PAD - HY_EMB), (0, 0)))
    full = lambda a: pl.BlockSpec(a.shape, lambda i: (0,) * a.ndim)
    ins = [hyena_features(n), w1p, b1.reshape(1, -1), w2, b2.reshape(1, -1), w3, b3.reshape(1, -1), freq, deltas]
    out_spec = pl.BlockSpec((tr, HY_WIDTH), lambda i: (i, 0))
    return pl.pallas_call(
        _hy_filter_kernel,
        grid=(n // tr,),
        in_specs=[pl.BlockSpec((tr, HY_EMB_PAD), lambda i: (i, 0))] + [full(a) for a in ins[1:]],
        out_specs=[out_spec, out_spec],
        out_shape=[jax.ShapeDtypeStruct((n, HY_WIDTH), F32)] * 2,
        compiler_params=_params("arbitrary"),
    )(*ins)


def _hy_outer_fwd_kernel(w_ref, u_ref, f_ref, b_ref, au_ref, af_ref, ab_ref):
    w = w_ref[...]
    for src, dst in ((u_ref, au_ref), (f_ref, af_ref), (b_ref, ab_ref)):
        dst[...] = jnp.dot(w, src[...].astype(BF16), preferred_element_type=F32).astype(BF16)


def _hy_mid_kernel(w2r_ref, w2i_ref, twr_ref, twi_ref, au_ref, af_ref, ab_ref, c_ref):
    n2 = HY_N2
    twr, twi = twr_ref[...], twi_ref[...]
    w2r, w2i = w2r_ref[...], w2i_ref[...]
    mr = w2r * twr - w2i * twi
    mi = w2r * twi + w2i * twr
    mblk = jnp.concatenate([jnp.concatenate([mr, -mi], axis=1), jnp.concatenate([mi, mr], axis=1)], axis=0).astype(BF16)
    x = jnp.concatenate([r[...].reshape(2 * n2, HY_WIDTH) for r in (au_ref, af_ref, ab_ref)], axis=1)
    s = jnp.dot(mblk, x, preferred_element_type=F32)
    W = HY_WIDTH
    ur, ui = s[:n2, 0:W], s[n2:, 0:W]
    hr = s[:n2, W:2 * W] + s[:n2, 2 * W:]
    hi = s[n2:, W:2 * W] - s[n2:, 2 * W:]
    p = jnp.concatenate([ur * hr - ui * hi, ur * hi + ui * hr], axis=0).astype(BF16)
    c = lax.dot_general(mblk, p, (((0,), (0,)), ((), ())), preferred_element_type=F32)
    c_ref[...] = c.astype(BF16).reshape(2, n2, W)


def _hy_outer_inv_kernel(w_ref, c_ref, x0_ref, u_ref, bias_ref, o_ref):
    conv = jnp.dot(w_ref[...], c_ref[...], preferred_element_type=F32)
    o_ref[...] = (x0_ref[...] * (conv + u_ref[...] * bias_ref[...])).astype(o_ref.dtype)


def _dft_tables(n1, n2):
    n = n1 * n2
    k1 = np.arange(n1)[:, None]
    t1 = np.arange(n1 // 2)[None, :]
    a1 = 2.0 * np.pi * k1 * t1 / n1
    w_fwd = np.concatenate([np.cos(a1), -np.sin(a1)], axis=0)
    w_inv = np.concatenate([np.cos(a1).T, -np.sin(a1).T], axis=1) / n
    a2 = 2.0 * np.pi * np.arange(n2)[:, None] * np.arange(n2)[None, :] / n2
    at = 2.0 * np.pi * np.arange(n1)[:, None] * np.arange(n2)[None, :] / n
    return (jnp.asarray(w_fwd, BF16), jnp.asarray(w_inv, BF16),
            jnp.asarray(np.cos(a2), F32), jnp.asarray(-np.sin(a2), F32),
            jnp.asarray(np.cos(at), F32).reshape(n1, 1, n2), jnp.asarray(-np.sin(at), F32).reshape(n1, 1, n2))


def hyena_long_conv(x0, u, ff, fb, bias):
    n, W = u.shape
    n2 = HY_N2
    n1 = 2 * n // n2
    assert 2 * n == n1 * n2 and n1 % (2 * SUBLANES) == 0
    w_fwd, w_inv, w2r, w2i, twr, twi = _dft_tables(n1, n2)
    cols = n2 * W
    assert cols % HY_COLS == 0
    view = lambda a: a.reshape(n1 // 2, cols)
    in_spec = pl.BlockSpec((n1 // 2, HY_COLS), lambda j: (0, j))
    sp_spec = pl.BlockSpec((2 * n1, HY_COLS), lambda j: (0, j))
    spec_shape = jax.ShapeDtypeStruct((2 * n1, cols), BF16)
    au, af, ab = pl.pallas_call(
        _hy_outer_fwd_kernel,
        grid=(cols // HY_COLS,),
        in_specs=[pl.BlockSpec((2 * n1, n1 // 2), lambda j: (0, 0)), in_spec, in_spec, in_spec],
        out_specs=[sp_spec] * 3,
        out_shape=[spec_shape] * 3,
        compiler_params=_params("arbitrary"),
    )(w_fwd, view(u), view(ff), view(fb))
    k_spec = pl.BlockSpec((2, None, n2, W), lambda k: (0, k, 0, 0))
    tw_spec = pl.BlockSpec((None, 1, n2), lambda k: (k, 0, 0))
    w2_spec = pl.BlockSpec((n2, n2), lambda k: (0, 0))
    as4 = lambda a: a.reshape(2, n1, n2, W)
    c = pl.pallas_call(
        _hy_mid_kernel,
        grid=(n1,),
        in_specs=[w2_spec, w2_spec, tw_spec, tw_spec, k_spec, k_spec, k_spec],
        out_specs=k_spec,
        out_shape=jax.ShapeDtypeStruct((2, n1, n2, W), BF16),
        compiler_params=_params("arbitrary"),
    )(w2r, w2i, twr, twi, as4(au), as4(af), as4(ab))
    bias_t = jnp.tile(bias.astype(F32), HY_COLS // W)[None, :]
    out = pl.pallas_call(
        _hy_outer_inv_kernel,
        grid=(cols // HY_COLS,),
        in_specs=[pl.BlockSpec((n1 // 2, 2 * n1), lambda j: (0, 0)), sp_spec, in_spec, in_spec,
                  pl.BlockSpec((1, HY_COLS), lambda j: (0, 0))],
        out_specs=in_spec,
        out_shape=jax.ShapeDtypeStruct((n1 // 2, cols), BF16),
        compiler_params=_params("arbitrary"),
    )(w_inv, c.reshape(2 * n1, cols), view(x0), view(u), bias_t)
    return out.reshape(n, W)


def _hy_small_kernel(f_ref, finv_ref, x0_ref, u_ref, ff_ref, fb_ref, bias_ref, o_ref):
    f = f_ref[...]
    nf = f.shape[0] // 2
    u = u_ref[...]
    spec = lambda a: jnp.dot(f, a.astype(BF16), preferred_element_type=F32)
    su, sf, sb = spec(u), spec(ff_ref[...]), spec(fb_ref[...])
    ur, ui = su[:nf], su[nf:]
    hr, hi = sf[:nf] + sb[:nf], sf[nf:] - sb[nf:]
    p = jnp.concatenate([ur * hr - ui * hi, ur * hi + ui * hr], axis=0).astype(BF16)
    conv = jnp.dot(finv_ref[...], p, preferred_element_type=F32)
    o_ref[...] = (x0_ref[...] * (conv + u * bias_ref[...])).astype(o_ref.dtype)


def hyena_short_conv(x0, u, ff, fb, bias):
    n, W = u.shape
    ang = 2.0 * np.pi * np.arange(2 * n)[:, None] * np.arange(n)[None, :] / (2 * n)
    f = jnp.asarray(np.concatenate([np.cos(ang), -np.sin(ang)], axis=0), BF16)
    finv = jnp.asarray(np.concatenate([np.cos(ang).T, -np.sin(ang).T], axis=1) / (2 * n), BF16)
    args = [f, finv, x0, u, ff, fb, bias.astype(F32)[None, :]]
    return pl.pallas_call(
        _hy_small_kernel,
        grid=(1,),
        in_specs=[pl.BlockSpec(a.shape, lambda i: (0, 0)) for a in args],
        out_specs=pl.BlockSpec((n, W), lambda i: (0, 0)),
        out_shape=jax.ShapeDtypeStruct((n, W), BF16),
        compiler_params=_params("arbitrary"),
    )(*args)


def hyena(z, short_w, short_b, filt, bias):
    n = z.shape[0]
    x0, u = hyena_pre(z, short_w, short_b)
    ff, fb = hyena_filter(n, *filt)
    if 2 * n >= HY_MIN_N1 * HY_N2:
        return hyena_long_conv(x0, u, ff, fb, bias)
    return hyena_short_conv(x0, u, ff, fb, bias)


def _row_tile(m):
    return 1024 if m % 1024 == 0 else m


def _channel_mixer(h, combine, w_gate, w_up, w_down):
    m = h.shape[0]
    act = expert_glu(h, w_gate, w_up, combine, tm=_row_tile(m), tn=256)
    w_down2 = w_down.reshape(w_down.shape[0] * w_down.shape[1], w_down.shape[2])
    k = w_down2.shape[0]
    return matmul(act, w_down2, tm=min(_row_tile(m), 4 * 1024 * 1024 // k), tn=512 if k <= 2048 else 256)


def kernel(x, c, ctx, c_ctx, ada_w, ada_b, norm_g, w_in, w_out, ret_decay, s5_a_re, s5_a_im, s5_log_dt, s5_b_re, s5_b_im, s5_c_re, s5_c_im, s5_d, s5_w_glu, hy_short_w, hy_short_b, hy_w1, hy_b1, hy_w2, hy_b2, hy_w3, hy_b3, hy_freq, hy_bias, ffn_w_gate, ffn_w_up, ffn_w_down, moe_w_router, moe_b_router, moe_w_gate, moe_w_up, moe_w_down):
    depth = ada_w.shape[0]
    xs = {0: x[0], 1: ctx[0]}
    n_lat, n_ctx = xs[0].shape[0], xs[1].shape[0]
    cond = jnp.concatenate([jax.nn.silu(c), jax.nn.silu(c_ctx)[None], jnp.zeros((SUBLANES - 2, D_MODEL), F32)], axis=0)
    mod = ada_modulation(cond.astype(BF16), ada_w, ada_b[:, None, :])
    mod = mod.reshape(depth, SUBLANES, N_MOD, D_MODEL).transpose(0, 2, 1, 3)
    rope = {0: rope_tables(n_lat),
            1: (jnp.ones((n_ctx, RET_DK), F32), jnp.zeros((n_ctx, RET_DK), F32))}
    ones = {r: jnp.ones((xs[r].shape[0], 1), F32) for r in xs}
    for l in range(depth):
        last = l == depth - 1
        g4, mod_l = norm_g[l], mod[l]
        log_gamma = jax.nn.log_sigmoid(ret_decay[l].astype(F32))
        tabs = s5_tables(s5_a_re[l], s5_a_im[l], s5_log_dt[l], s5_b_re[l], s5_b_im[l], s5_c_re[l], s5_c_im[l])
        filt = (hy_w1[l], hy_b1[l], hy_w2[l], hy_b2[l], hy_w3[l], hy_b3[l], hy_freq[l])
        j = l // 2
        if l % 2 == 1:
            router = (jnp.pad(moe_w_router[j].astype(F32), ((0, 0), (0, ROUTER_PAD - N_EXPERTS))),
                      jnp.concatenate([moe_b_router[j].astype(F32), jnp.full((ROUTER_PAD - N_EXPERTS,), NEG_BIG, F32)])[None, :])
        else:
            router = None
        ret_state = jnp.zeros((2, RET_HEADS, RET_DK, RET_DV), F32)
        s5_state = jnp.zeros((2, SUBLANES, S5_LANES), F32)
        for r in (1, 0):
            xr = xs[r]
            m = xr.shape[0]
            h = prenorm(xr, g4, mod_l, r)
            z = matmul(h, w_in[l], tm=_row_tile(m), tn=512)
            y_ret, ret_fin = retention(z, log_gamma, rope[r][0], rope[r][1], ret_state)
            y_s5, s5_fin = s5(z, tabs, s5_state, s5_d[l][None, :], s5_w_glu[l])
            if r == 1:
                ret_state, s5_state = ret_fin, s5_fin
                if last:
                    continue
            y_hy = hyena(z, hy_short_w[l], hy_short_b[l], filt, hy_bias[l])
            y = matmul(jnp.concatenate([y_ret, y_s5, y_hy], axis=1), w_out[l], tm=_row_tile(m), tn=512)
            if router is None:
                xr, h = resid_update(xr, y, g4, mod_l, r, 0, has_next=True)
                f = _channel_mixer(h, ones[r], ffn_w_gate[j][None], ffn_w_up[j][None], ffn_w_down[j][None])
            else:
                xr, h, combine = resid_update(xr, y, g4, mod_l, r, 0, has_next=True, router=router)
                f = _channel_mixer(h, combine, moe_w_gate[j], moe_w_up[j], moe_w_down[j])
            xs[r] = resid_update(xr, f, g4, mod_l, r, 1, has_next=False)[0]
    return xs[0][None]
```

```python
import functools
import math

import jax
import jax.numpy as jnp
import numpy as np
from jax import lax
from jax.experimental import pallas as pl
from jax.experimental.pallas import tpu as pltpu

D_MODEL = 4096
GRID_W = 64
N_MOD = 6
EPS = 1e-6

RET_HEADS = 8
RET_DK = 128
RET_DV = 256
RET_QK = RET_HEADS * RET_DK
RET_V = RET_HEADS * RET_DV
RET_CHUNK = 128
ROPE_BASE = 10000.0

S5_WIDTH = 1024
S5_GROUP = 16
S5_GROUPS = S5_WIDTH // S5_GROUP
S5_STATE = 64
S5_MAX_RE = -1e-4
S5_LANES = S5_GROUPS * S5_STATE
S5_BLK_GROUPS = 8
S5_BLK = S5_BLK_GROUPS * S5_STATE
S5_NBLK = S5_GROUPS // S5_BLK_GROUPS

HY_WIDTH = 1024
HY_ORDER = 2
HY_EMB = 33
HY_EMB_PAD = 64
HY_BANDS = (HY_EMB - 1) // 2
HY_FAST_DECAY = 0.3
HY_SLOW_DECAY = 1.5
HY_TARGET = 1e-2
HY_N2 = 256
HY_MIN_N1 = 16

COL_Q, COL_K, COL_V, COL_G = 0, RET_QK, 2 * RET_QK, 2 * RET_QK + RET_V
COL_U = 2 * RET_QK + 2 * RET_V
COL_Z = COL_U + S5_WIDTH

N_EXPERTS = 8
ROUTER_PAD = 128
NEG_BIG = -1e30

BF16 = jnp.bfloat16
F32 = jnp.float32
HIGHEST = lax.Precision.HIGHEST

VMEM_LIMIT_BYTES = 56 * 1024 * 1024
CAST_ROWS = 256
SUBLANES = 8
LANES = 128


def _params(*sem):
    return pltpu.CompilerParams(dimension_semantics=sem, vmem_limit_bytes=VMEM_LIMIT_BYTES)


def _cast_weight(w_ref, wb_ref):
    k = w_ref.shape[0]
    rows = min(CAST_ROWS, k)

    def body(c, carry):
        r = pl.multiple_of(c * rows, rows)
        wb_ref[pl.ds(r, rows), :] = w_ref[pl.ds(r, rows), :].astype(BF16)
        return carry

    lax.fori_loop(0, k // rows, body, 0)


def _mm_kernel(*refs, widths):
    a_refs = refs[:len(widths)]
    w_ref, o_ref, wb_ref = refs[len(widths):]

    @pl.when(pl.program_id(1) == 0)
    def _():
        _cast_weight(w_ref, wb_ref)

    acc, off = None, 0
    for a_ref, wd in zip(a_refs, widths):
        part = jnp.dot(a_ref[...], wb_ref[off:off + wd, :], preferred_element_type=F32)
        acc = part if acc is None else acc + part
        off += wd
    o_ref[...] = acc.astype(o_ref.dtype)


def matmul(a_parts, w, layer, *, tm, tn, out_dtype=F32):
    a_parts = list(a_parts)
    m = a_parts[0].shape[0]
    widths = tuple(a.shape[1] for a in a_parts)
    _, k, n = w.shape
    assert sum(widths) == k and m % tm == 0 and n % tn == 0, (widths, w.shape, tm, tn)
    return pl.pallas_call(
        functools.partial(_mm_kernel, widths=widths),
        grid=(n // tn, m // tm),
        in_specs=[pl.BlockSpec((tm, wd), lambda j, i: (i, 0)) for wd in widths]
                 + [pl.BlockSpec((None, k, tn), lambda j, i: (layer, 0, j))],
        out_specs=pl.BlockSpec((tm, tn), lambda j, i: (i, j)),
        out_shape=jax.ShapeDtypeStruct((m, n), out_dtype),
        scratch_shapes=[pltpu.VMEM((k, tn), BF16)],
        compiler_params=_params("arbitrary", "arbitrary"),
    )(*a_parts, w)


def _glu_kernel(a_ref, wg_ref, wu_ref, c_ref, o_ref, wgb_ref, wub_ref):
    @pl.when(pl.program_id(2) == 0)
    def _():
        _cast_weight(wg_ref, wgb_ref)
        _cast_weight(wu_ref, wub_ref)

    a = a_ref[...]
    g = jnp.dot(a, wgb_ref[...], preferred_element_type=F32)
    u = jnp.dot(a, wub_ref[...], preferred_element_type=F32)
    act = g * jax.nn.sigmoid(g) * u
    e = pl.program_id(0)
    c = c_ref[...]
    lane = lax.broadcasted_iota(jnp.int32, c.shape, 1)
    ce = jnp.sum(jnp.where(lane == e, c, 0.0), axis=1, keepdims=True)
    o_ref[...] = (act * ce).astype(o_ref.dtype)


def expert_glu(a, w_gate, w_up, combine, e0, e, *, tm, tn):
    m, k = a.shape
    _, k2, f = w_gate.shape
    assert k == k2 and m % tm == 0 and f % tn == 0
    nj = f // tn
    return pl.pallas_call(
        _glu_kernel,
        grid=(e, nj, m // tm),
        in_specs=[pl.BlockSpec((tm, k), lambda e_, j, i: (i, 0)),
                  pl.BlockSpec((None, k, tn), lambda e_, j, i: (e0 + e_, 0, j)),
                  pl.BlockSpec((None, k, tn), lambda e_, j, i: (e0 + e_, 0, j)),
                  pl.BlockSpec((tm, combine.shape[1]), lambda e_, j, i: (i, 0))],
        out_specs=pl.BlockSpec((tm, tn), lambda e_, j, i: (i, e_ * nj + j)),
        out_shape=jax.ShapeDtypeStruct((m, e * f), BF16),
        scratch_shapes=[pltpu.VMEM((k, tn), BF16), pltpu.VMEM((k, tn), BF16)],
        compiler_params=_params("arbitrary", "arbitrary", "arbitrary"),
    )(a, w_gate, w_up, combine)


def _ada_kernel(a_ref, w_ref, b_ref, o_ref):
    o_ref[...] = jnp.dot(a_ref[...], w_ref[...].astype(BF16), preferred_element_type=F32) + b_ref[...]


def ada_modulation(cond, ada_w, ada_b, *, tn=1024):
    depth, d, n = ada_w.shape
    return pl.pallas_call(
        _ada_kernel,
        grid=(depth, n // tn),
        in_specs=[pl.BlockSpec((SUBLANES, d), lambda l, j: (0, 0)),
                  pl.BlockSpec((None, d, tn), lambda l, j: (l, 0, j)),
                  pl.BlockSpec((None, 1, tn), lambda l, j: (l, 0, j))],
        out_specs=pl.BlockSpec((None, SUBLANES, tn), lambda l, j: (l, 0, j)),
        out_shape=jax.ShapeDtypeStruct((depth, SUBLANES, n), F32),
        compiler_params=_params("arbitrary", "arbitrary"),
    )(cond, ada_w, ada_b)


NORM_ROWS = 256


def _rms(x):
    return x * lax.rsqrt(jnp.mean(x * x, axis=-1, keepdims=True) + EPS)


def _modulated(x, g, mod_ref, row, shift_idx):
    shift = mod_ref[shift_idx, row:row + 1, :]
    scale = mod_ref[shift_idx + 1, row:row + 1, :]
    return _rms(x) * g * (1.0 + scale) + shift


def _prenorm_kernel(x_ref, g_ref, mod_ref, h_ref, *, row):
    h_ref[...] = _modulated(x_ref[...], g_ref[0:1, :], mod_ref, row, 0).astype(BF16)


def prenorm(x, g4, mod_l, row):
    m, d = x.shape
    tr = min(NORM_ROWS, m)
    return pl.pallas_call(
        functools.partial(_prenorm_kernel, row=row),
        grid=(m // tr,),
        in_specs=[pl.BlockSpec((tr, d), lambda i: (i, 0)),
                  pl.BlockSpec((4, d), lambda i: (0, 0)),
                  pl.BlockSpec((N_MOD, SUBLANES, d), lambda i: (0, 0, 0))],
        out_specs=pl.BlockSpec((tr, d), lambda i: (i, 0)),
        out_shape=jax.ShapeDtypeStruct((m, d), BF16),
        compiler_params=_params("arbitrary"),
    )(x, g4, mod_l)


def _top2_combine(logits):
    lane = lax.broadcasted_iota(jnp.int32, logits.shape, 1)
    big = jnp.int32(logits.shape[1])
    v1 = jnp.max(logits, axis=1, keepdims=True)
    i1 = jnp.min(jnp.where(logits == v1, lane, big), axis=1, keepdims=True)
    rest = jnp.where(lane == i1, NEG_BIG, logits)
    v2 = jnp.max(rest, axis=1, keepdims=True)
    i2 = jnp.min(jnp.where(rest == v2, lane, big), axis=1, keepdims=True)
    e2 = jnp.exp(v2 - v1)
    w1 = 1.0 / (1.0 + e2)
    w2 = e2 * w1
    return jnp.where(lane == i1, w1, 0.0) + jnp.where(lane == i2, w2, 0.0)


def _resid_kernel(*refs, row, stage, has_next, has_router):
    x_ref, y_ref, g_ref, mod_ref = refs[:4]
    pos = 4
    if has_router:
        wr_ref, br_ref = refs[pos:pos + 2]
        pos += 2
    xo_ref = refs[pos]
    pos += 1
    gate = mod_ref[3 * stage + 2, row:row + 1, :]
    xn = x_ref[...] + gate * (_rms(y_ref[...]) * g_ref[2 * stage + 1:2 * stage + 2, :])
    xo_ref[...] = xn
    if has_next:
        h_ref = refs[pos]
        pos += 1
        h = _modulated(xn, g_ref[2:3, :], mod_ref, row, 3)
        h_ref[...] = h.astype(BF16)
        if has_router:
            c_ref = refs[pos]
            logits = jnp.dot(h, wr_ref[...], preferred_element_type=F32, precision=HIGHEST) + br_ref[...]
            c_ref[...] = _top2_combine(logits)


def resid_update(x, y, g4, mod_l, row, stage, *, has_next, router=None):
    m, d = x.shape
    tr = min(NORM_ROWS, m)
    row_spec = pl.BlockSpec((tr, d), lambda i: (i, 0))
    in_specs = [row_spec, row_spec, pl.BlockSpec((4, d), lambda i: (0, 0)),
                pl.BlockSpec((N_MOD, SUBLANES, d), lambda i: (0, 0, 0))]
    args = [x, y, g4, mod_l]
    out_specs = [row_spec]
    out_shape = [jax.ShapeDtypeStruct((m, d), F32)]
    if router is not None:
        in_specs += [pl.BlockSpec((d, ROUTER_PAD), lambda i: (0, 0)), pl.BlockSpec((1, ROUTER_PAD), lambda i: (0, 0))]
        args += list(router)
    if has_next:
        out_specs.append(row_spec)
        out_shape.append(jax.ShapeDtypeStruct((m, d), BF16))
        if router is not None:
            out_specs.append(pl.BlockSpec((tr, ROUTER_PAD), lambda i: (i, 0)))
            out_shape.append(jax.ShapeDtypeStruct((m, ROUTER_PAD), F32))
    return pl.pallas_call(
        functools.partial(_resid_kernel, row=row, stage=stage, has_next=has_next, has_router=router is not None),
        grid=(m // tr,),
        in_specs=in_specs,
        out_specs=out_specs,
        out_shape=out_shape,
        compiler_params=_params("arbitrary"),
    )(*args)


def _rope(x, cos, sin):
    return x * cos + pltpu.roll(x, RET_DK // 2, axis=1) * sin


def _ret_kernel(*refs, nsteps, nchunks, reverse):
    if reverse:
        lg_ref, q_ref, k_ref, v_ref, cos_ref, sin_ref, r0_ref, o_ref, rfin_ref, r_ref = refs
    else:
        lg_ref, q_ref, k_ref, v_ref, cos_ref, sin_ref, r0_ref, g_ref, ob_ref, o_ref, rfin_ref, r_ref = refs
    s = pl.program_id(0)
    C = RET_CHUNK

    @pl.when(s == 0)
    def _():
        r_ref[...] = r0_ref[...]

    n_i = lax.broadcasted_iota(jnp.int32, (C, C), 0)
    m_i = lax.broadcasted_iota(jnp.int32, (C, C), 1)
    dist = ((m_i - n_i) if reverse else (n_i - m_i)).astype(F32)
    valid = (dist > 0) if reverse else (dist >= 0)
    pos = lax.broadcasted_iota(jnp.int32, (C, 1), 0).astype(F32)
    to_state = (pos if reverse else (C - 1 - pos))
    from_state = ((C - pos) if reverse else (pos + 1))
    scale = RET_DK ** -0.5

    for h in range(RET_HEADS):
        lg = lg_ref[1 if reverse else 0, h]
        decay = jnp.where(valid, jnp.exp(lg * jnp.maximum(dist, 0.0)), 0.0)
        xi = jnp.exp(lg * from_state)
        zeta = jnp.exp(lg * to_state)
        chunk_decay = jnp.exp(lg * C)
        qc = slice(h * RET_DK, (h + 1) * RET_DK)
        vc = slice(h * RET_DV, (h + 1) * RET_DV)
        for j in (range(nchunks - 1, -1, -1) if reverse else range(nchunks)):
            rows = slice(j * C, (j + 1) * C)
            cos, sin = cos_ref[rows, :], sin_ref[rows, :]
            q = _rope(q_ref[rows, qc], cos, sin)
            k = _rope(k_ref[rows, qc], cos, sin) * scale
            vb = v_ref[rows, vc].astype(BF16)
            qb = q.astype(BF16)
            scores = lax.dot_general(qb, k.astype(BF16), (((1,), (1,)), ((), ())), preferred_element_type=F32)
            inner = jnp.dot((scores * decay).astype(BF16), vb, preferred_element_type=F32)
            state = r_ref[h]
            o = inner + jnp.dot(qb, state.astype(BF16), preferred_element_type=F32) * xi
            kv = lax.dot_general((k * zeta).astype(BF16), vb, (((0,), (0,)), ((), ())), preferred_element_type=F32)
            r_ref[h] = chunk_decay * state + kv
            if reverse:
                o_ref[rows, vc] = o.astype(o_ref.dtype)
            else:
                t = o + ob_ref[rows, vc].astype(F32)
                t = t * lax.rsqrt(jnp.mean(t * t, axis=-1, keepdims=True) + EPS)
                g = g_ref[rows, vc]
                o_ref[rows, vc] = (t * (g * jax.nn.sigmoid(g))).astype(o_ref.dtype)

    @pl.when(s == nsteps - 1)
    def _():
        rfin_ref[...] = r_ref[...]


def retention(z, log_gamma, rope_cos, rope_sin, r0):
    L = z.shape[0]
    nchunks = min(4, L // RET_CHUNK)
    rows = nchunks * RET_CHUNK
    nsteps = L // rows

    def call(reverse, extra, extra_specs):
        d = 1 if reverse else 0
        rmap = (lambda c: (lambda s: (nsteps - 1 - s, c))) if reverse else (lambda c: (lambda s: (s, c)))
        st_spec = pl.BlockSpec((None, RET_HEADS, RET_DK, RET_DV), lambda s: (d, 0, 0, 0))
        return pl.pallas_call(
            functools.partial(_ret_kernel, nsteps=nsteps, nchunks=nchunks, reverse=reverse),
            grid=(nsteps,),
            in_specs=[pl.BlockSpec(memory_space=pltpu.SMEM),
                      pl.BlockSpec((rows, RET_QK), rmap(COL_Q // RET_QK)),
                      pl.BlockSpec((rows, RET_QK), rmap(COL_K // RET_QK)),
                      pl.BlockSpec((rows, RET_V), rmap(COL_V // RET_V)),
                      pl.BlockSpec((rows, RET_DK), rmap(0)), pl.BlockSpec((rows, RET_DK), rmap(0)),
                      st_spec] + [pl.BlockSpec((rows, RET_V), rmap(c)) for c in extra_specs],
            out_specs=[pl.BlockSpec((rows, RET_V), rmap(0)),
                       pl.BlockSpec((RET_HEADS, RET_DK, RET_DV), lambda s: (0, 0, 0))],
            out_shape=[jax.ShapeDtypeStruct((L, RET_V), BF16),
                       jax.ShapeDtypeStruct((RET_HEADS, RET_DK, RET_DV), F32)],
            scratch_shapes=[pltpu.VMEM((RET_HEADS, RET_DK, RET_DV), F32)],
            compiler_params=_params("arbitrary"),
        )(log_gamma, z, z, z, rope_cos, rope_sin, r0, *extra)

    o_b, rfin_b = call(True, [], [])
    o, rfin_f = call(False, [z, o_b], [COL_G // RET_V, 0])
    return o, jnp.stack([rfin_f, rfin_b])


def rope_tables(n_tokens):
    rows = n_tokens // GRID_W
    row = jnp.repeat(jnp.arange(rows, dtype=F32), GRID_W)
    col = jnp.tile(jnp.arange(GRID_W, dtype=F32), rows)
    n_freq = RET_DK // 4
    inv = ROPE_BASE ** (-jnp.arange(n_freq, dtype=F32) / n_freq)
    ang = jnp.concatenate([row[:, None] * inv, col[:, None] * inv], axis=-1)
    cos, sin = jnp.cos(ang), jnp.sin(ang)
    return jnp.concatenate([cos, cos], axis=-1), jnp.concatenate([-sin, sin], axis=-1)


S5_ROWS = 256


def _s5_scan_block(x_re, x_im, pw, c_re, c_im, reverse):
    for d, (m_re, m_im) in zip((1, 2, 4), pw["steps"]):
        shift = (SUBLANES - d) if reverse else d
        s_re = pltpu.roll(x_re, shift, axis=0)
        s_im = pltpu.roll(x_im, shift, axis=0)
        x_re, x_im = x_re + (m_re * s_re - m_im * s_im), x_im + (m_re * s_im + m_im * s_re)
    e_re, e_im = pw["edge"]
    x_re = x_re + (e_re * c_re - e_im * c_im)
    x_im = x_im + (e_re * c_im + e_im * c_re)
    return x_re, x_im


def _s5_kernel(*refs, nsteps, reverse):
    if reverse:
        u_ref, bmat_ref, cmat_ref, pow_ref, x0_ref, y_out_ref, xfin_ref, bu_ref, carry_ref = refs
    else:
        (u_ref, bmat_ref, cmat_ref, pow_ref, x0_ref, d_ref, wglu_ref, yb_ref,
         out_ref, xfin_ref, bu_ref, carry_ref, y_ref, wb_ref) = refs
    s = pl.program_id(0)
    rows = u_ref.shape[0]
    nb = rows // SUBLANES
    W = S5_BLK

    @pl.when(s == 0)
    def _():
        carry_ref[...] = x0_ref[...]
        if not reverse:
            _cast_weight(wglu_ref, wb_ref)

    u = u_ref[...]
    ub = u.astype(BF16)
    y_dst = y_out_ref if reverse else y_ref
    for j in range(S5_NBLK):
        bu_ref[...] = jnp.dot(ub[:, j * LANES:(j + 1) * LANES], bmat_ref[j], preferred_element_type=F32)
        lo = j * W
        pw = {
            "steps": [(pow_ref[2 * t, :, lo:lo + W], pow_ref[2 * t + 1, :, lo:lo + W]) for t in range(3)],
            "edge": (pow_ref[6, :, lo:lo + W], pow_ref[7, :, lo:lo + W]),
        }
        c0 = (carry_ref[0:1, lo:lo + W], carry_ref[1:2, lo:lo + W])

        def body(b, c, pw=pw):
            bi = (nb - 1 - b) if reverse else b
            r = pl.multiple_of(bi * SUBLANES, SUBLANES)
            x_re, x_im = _s5_scan_block(bu_ref[pl.ds(r, SUBLANES), 0:W], bu_ref[pl.ds(r, SUBLANES), W:2 * W],
                                        pw, c[0], c[1], reverse)
            bu_ref[pl.ds(r, SUBLANES), 0:W] = x_re
            bu_ref[pl.ds(r, SUBLANES), W:2 * W] = x_im
            e = 0 if reverse else SUBLANES - 1
            return x_re[e:e + 1, :], x_im[e:e + 1, :]

        c_re, c_im = lax.fori_loop(0, nb, body, c0)
        carry_ref[0:1, lo:lo + W] = c_re
        carry_ref[1:2, lo:lo + W] = c_im
        y_dst[:, j * LANES:(j + 1) * LANES] = jnp.dot(bu_ref[...].astype(BF16), cmat_ref[j],
                                                      preferred_element_type=F32)

    if not reverse:
        y = y_ref[...] + yb_ref[...] + d_ref[...] * u
        z = jnp.dot(jax.nn.gelu(y).astype(BF16), wb_ref[...], preferred_element_type=F32)
        out_ref[...] = (z[:, :S5_WIDTH] * jax.nn.sigmoid(z[:, S5_WIDTH:])).astype(out_ref.dtype)

    @pl.when(s == nsteps - 1)
    def _():
        xfin_ref[...] = carry_ref[...]


def s5_tables(a_re, a_im, log_dt, b_re, b_im, c_re, c_im):
    lam = lax.complex(jnp.minimum(a_re.astype(F32), S5_MAX_RE), a_im.astype(F32))
    lam_bar = jnp.exp(lam * jnp.exp(log_dt.astype(F32))[..., None])
    b_mat = lax.complex(b_re.astype(F32), b_im.astype(F32))
    b_bar = ((lam_bar - 1.0) / lam)[..., None] * b_mat[None]
    eye = jnp.eye(S5_BLK_GROUPS, dtype=F32)

    def in_map(m):
        m = m.reshape(S5_NBLK, S5_BLK_GROUPS, S5_STATE, S5_GROUP)
        return jnp.einsum('jgph,gk->jghkp', m, eye).reshape(S5_NBLK, LANES, S5_BLK)

    bmat = jnp.stack([jnp.concatenate([in_map(jnp.real(b_bar[d])), in_map(jnp.imag(b_bar[d]))], axis=-1)
                      for d in range(2)]).astype(BF16)

    def out_map(m):
        m = m.reshape(S5_NBLK, S5_BLK_GROUPS, S5_GROUP, S5_STATE)
        return jnp.einsum('jghp,gk->jgpkh', m, eye).reshape(S5_NBLK, S5_BLK, LANES)

    cmat = jnp.concatenate([out_map(c_re.astype(F32)), out_map(-c_im.astype(F32))], axis=1).astype(BF16)

    lam_flat = lam_bar.reshape(2, 1, S5_LANES)
    r = jnp.arange(SUBLANES)[None, :, None]

    def masked_pow(d, reverse):
        ok = (r + d < SUBLANES) if reverse else (r >= d)
        return jnp.where(ok, lam_flat[1 if reverse else 0] ** d, 0.0)

    def tables(reverse):
        rows = []
        for d in (1, 2, 4):
            m = masked_pow(d, reverse)
            rows += [jnp.real(m), jnp.imag(m)]
        dist = (SUBLANES - r) if reverse else (r + 1)
        e = lam_flat[1 if reverse else 0] ** dist
        rows += [jnp.real(e), jnp.imag(e)]
        return jnp.stack([jnp.broadcast_to(t, (1, SUBLANES, S5_LANES))[0] for t in rows])

    pows = jnp.stack([tables(False), tables(True)]).astype(F32)
    return bmat, cmat, pows


def s5(z, tables, x0, d_skip, w_glu, layer):
    bmat, cmat, pows = tables
    L = z.shape[0]
    rows = min(S5_ROWS, L)
    nsteps = L // rows
    ucol = COL_U // S5_WIDTH

    def call(reverse, extra_in, extra_specs, out_dtype, scratch):
        d = 1 if reverse else 0
        rmap = (lambda s: (nsteps - 1 - s, 0)) if reverse else (lambda s: (s, 0))
        umap = (lambda s: (nsteps - 1 - s, ucol)) if reverse else (lambda s: (s, ucol))
        st_spec = pl.BlockSpec((None, SUBLANES, S5_LANES), lambda s: (d, 0, 0))
        return pl.pallas_call(
            functools.partial(_s5_kernel, nsteps=nsteps, reverse=reverse),
            grid=(nsteps,),
            in_specs=[pl.BlockSpec((rows, S5_WIDTH), umap),
                      pl.BlockSpec((None, S5_NBLK, LANES, 2 * S5_BLK), lambda s: (d, 0, 0, 0)),
                      pl.BlockSpec((S5_NBLK, 2 * S5_BLK, LANES), lambda s: (0, 0, 0)),
                      pl.BlockSpec((None, 8, SUBLANES, S5_LANES), lambda s: (d, 0, 0, 0)),
                      st_spec] + [sp(rmap) for sp in extra_specs],
            out_specs=[pl.BlockSpec((rows, S5_WIDTH), rmap), pl.BlockSpec((SUBLANES, S5_LANES), lambda s: (0, 0))],
            out_shape=[jax.ShapeDtypeStruct((L, S5_WIDTH), out_dtype),
                       jax.ShapeDtypeStruct((SUBLANES, S5_LANES), F32)],
            scratch_shapes=[pltpu.VMEM((rows, 2 * S5_BLK), F32), pltpu.VMEM((SUBLANES, S5_LANES), F32)] + scratch,
            compiler_params=_params("arbitrary"),
        )(z, bmat, cmat, pows, x0, *extra_in)

    yb, xfin_b = call(True, [], [], F32, [])
    out, xfin_f = call(
        False, [d_skip, w_glu, yb],
        [lambda rmap: pl.BlockSpec((1, S5_WIDTH), lambda s: (0, 0)),
         lambda rmap: pl.BlockSpec((None, S5_WIDTH, 2 * S5_WIDTH), lambda s: (layer, 0, 0)),
         lambda rmap: pl.BlockSpec((rows, S5_WIDTH), rmap)],
        BF16,
        [pltpu.VMEM((rows, S5_WIDTH), F32), pltpu.VMEM((S5_WIDTH, 2 * S5_WIDTH), BF16)])
    return out, jnp.stack([xfin_f, xfin_b])


HY_ROWS = 512
HY_COLS = 8192


def _hy_pre_kernel(*refs, nsteps):
    mains, prevs, nexts = refs[0:3], refs[3:6], refs[6:9]
    sws, sbs = refs[9:12], refs[12:15]
    x0_ref, u_ref = refs[15:17]
    i = pl.program_id(0)
    tr = mains[0].shape[0]
    row = lax.broadcasted_iota(jnp.int32, (tr, 1), 0)
    parts = []
    for c in range(HY_ORDER + 1):
        z = mains[c][...]
        before = jnp.where(i == 0, 0.0, prevs[c][SUBLANES - 1:SUBLANES, :])
        after = jnp.where(i == nsteps - 1, 0.0, nexts[c][0:1, :])
        down = jnp.where(row == 0, before, pltpu.roll(z, 1, axis=0))
        up = jnp.where(row == tr - 1, after, pltpu.roll(z, tr - 1, axis=0))
        sw = sws[c]
        parts.append(sw[0:1, :] * down + sw[1:2, :] * z + sw[2:3, :] * up + sbs[c][...])
    x0_ref[...] = parts[0]
    u_ref[...] = parts[1] * parts[2]


def hyena_pre(z, short_w, short_b):
    L = z.shape[0]
    tr = min(HY_ROWS, L)
    nsteps = L // tr
    per = tr // SUBLANES
    last8 = L // SUBLANES - 1
    cb = COL_Z // HY_WIDTH
    specs, args = [], []
    for kind in range(3):
        for c in range(HY_ORDER + 1):
            if kind == 0:
                specs.append(pl.BlockSpec((tr, HY_WIDTH), lambda i, c=c: (i, cb + c)))
            elif kind == 1:
                specs.append(pl.BlockSpec((SUBLANES, HY_WIDTH), lambda i, c=c: (jnp.maximum(i * per - 1, 0), cb + c)))
            else:
                specs.append(pl.BlockSpec((SUBLANES, HY_WIDTH), lambda i, c=c: (jnp.minimum((i + 1) * per, last8), cb + c)))
            args.append(z)
    for c in range(HY_ORDER + 1):
        specs.append(pl.BlockSpec((3, HY_WIDTH), lambda i, c=c: (0, c)))
        args.append(short_w)
    sb = short_b.reshape(1, -1)
    for c in range(HY_ORDER + 1):
        specs.append(pl.BlockSpec((1, HY_WIDTH), lambda i, c=c: (0, c)))
        args.append(sb)
    out_spec = pl.BlockSpec((tr, HY_WIDTH), lambda i: (i, 0))
    return pl.pallas_call(
        functools.partial(_hy_pre_kernel, nsteps=nsteps),
        grid=(nsteps,),
        in_specs=specs,
        out_specs=[out_spec, out_spec],
        out_shape=[jax.ShapeDtypeStruct((L, HY_WIDTH), F32)] * 2,
        compiler_params=_params("arbitrary"),
    )(*args)


def _hy_filter_kernel(z_ref, w1_ref, b1_ref, w2_ref, b2_ref, w3_ref, b3_ref, fr_ref, dl_ref, ff_ref, fb_ref):
    i = pl.program_id(0)
    z = z_ref[...]
    tr = z.shape[0]
    h = jnp.sin(fr_ref[0:1, :] * (jnp.dot(z, w1_ref[...], preferred_element_type=F32, precision=HIGHEST) + b1_ref[...]))
    h = jnp.sin(fr_ref[1:2, :] * (jnp.dot(h, w2_ref[...], preferred_element_type=F32, precision=HIGHEST) + b2_ref[...]))
    h = jnp.dot(h, w3_ref[...], preferred_element_type=F32, precision=HIGHEST) + b3_ref[...]
    decay = jnp.exp(-z[:, 0:1] * dl_ref[...])
    ff_ref[...] = h[:, :HY_WIDTH] * decay
    row = i * tr + lax.broadcasted_iota(jnp.int32, (tr, 1), 0)
    fb_ref[...] = jnp.where(row == 0, 0.0, h[:, HY_WIDTH:] * decay)


def hyena_features(n):
    t = jnp.linspace(0.0, 1.0, n, dtype=F32)[:, None]
    w = 2.0 * math.pi * jnp.arange(n, dtype=F32)[:, None] / n
    f = jnp.linspace(1e-4, HY_BANDS - 1, HY_BANDS, dtype=F32)[None, :]
    z = jnp.concatenate([t, jnp.cos(f * w), -jnp.sin(f * w)], axis=-1)
    return jnp.pad(z, ((0, 0), (0, HY_EMB_PAD - HY_EMB)))


def hyena_filter(n, w1, b1, w2, b2, w3, b3, freq):
    tr = min(HY_ROWS, n)
    max_decay = math.log(HY_TARGET) / HY_FAST_DECAY
    min_decay = math.log(HY_TARGET) / HY_SLOW_DECAY
    deltas = jnp.abs(jnp.linspace(min_decay, max_decay, HY_WIDTH, dtype=F32))[None, :]
    w1p = jnp.pad(w1.astype(F32), ((0, HY_EMB_PAD - HY_EMB), (0, 0)))
    full = lambda a: pl.BlockSpec(a.shape, lambda i: (0,) * a.ndim)
    ins = [hyena_features(n), w1p, b1.reshape(1, -1), w2, b2.reshape(1, -1), w3, b3.reshape(1, -1), freq, deltas]
    out_spec = pl.BlockSpec((tr, HY_WIDTH), lambda i: (i, 0))
    return pl.pallas_call(
        _hy_filter_kernel,
        grid=(n // tr,),
        in_specs=[pl.BlockSpec((tr, HY_EMB_PAD), lambda i: (i, 0))] + [full(a) for a in ins[1:]],
        out_specs=[out_spec, out_spec],
        out_shape=[jax.ShapeDtypeStruct((n, HY_WIDTH), F32)] * 2,
        compiler_params=_params("arbitrary"),
    )(*ins)


def _hy_outer_fwd_kernel(w_ref, u_ref, f_ref, b_ref, au_ref, af_ref, ab_ref):
    w = w_ref[...]
    for src, dst in ((u_ref, au_ref), (f_ref, af_ref), (b_ref, ab_ref)):
        dst[...] = jnp.dot(w, src[...].astype(BF16), preferred_element_type=F32).astype(BF16)


def _hy_mid_kernel(w2r_ref, w2i_ref, twr_ref, twi_ref, au_ref, af_ref, ab_ref, c_ref):
    n2 = HY_N2
    twr, twi = twr_ref[...], twi_ref[...]
    w2r, w2i = w2r_ref[...], w2i_ref[...]
    mr = w2r * twr - w2i * twi
    mi = w2r * twi + w2i * twr
    mblk = jnp.concatenate([jnp.concatenate([mr, -mi], axis=1), jnp.concatenate([mi, mr], axis=1)], axis=0).astype(BF16)
    x = jnp.concatenate([r[...].reshape(2 * n2, HY_WIDTH) for r in (au_ref, af_ref, ab_ref)], axis=1)
    s = jnp.dot(mblk, x, preferred_element_type=F32)
    W = HY_WIDTH
    ur, ui = s[:n2, 0:W], s[n2:, 0:W]
    hr = s[:n2, W:2 * W] + s[:n2, 2 * W:]
    hi = s[n2:, W:2 * W] - s[n2:, 2 * W:]
    p = jnp.concatenate([ur * hr - ui * hi, ur * hi + ui * hr], axis=0).astype(BF16)
    c = lax.dot_general(mblk, p, (((0,), (0,)), ((), ())), preferred_element_type=F32)
    c_ref[...] = c.astype(BF16).reshape(2, n2, W)


def _hy_outer_inv_kernel(w_ref, c_ref, x0_ref, u_ref, bias_ref, o_ref):
    conv = jnp.dot(w_ref[...], c_ref[...], preferred_element_type=F32)
    o_ref[...] = (x0_ref[...] * (conv + u_ref[...] * bias_ref[...])).astype(o_ref.dtype)


def _dft_tables(n1, n2):
    n = n1 * n2
    k1 = np.arange(n1)[:, None]
    t1 = np.arange(n1 // 2)[None, :]
    a1 = 2.0 * np.pi * k1 * t1 / n1
    w_fwd = np.concatenate([np.cos(a1), -np.sin(a1)], axis=0)
    w_inv = np.concatenate([np.cos(a1).T, -np.sin(a1).T], axis=1) / n
    a2 = 2.0 * np.pi * np.arange(n2)[:, None] * np.arange(n2)[None, :] / n2
    at = 2.0 * np.pi * np.arange(n1)[:, None] * np.arange(n2)[None, :] / n
    return (jnp.asarray(w_fwd, BF16), jnp.asarray(w_inv, BF16),
            jnp.asarray(np.cos(a2), F32), jnp.asarray(-np.sin(a2), F32),
            jnp.asarray(np.cos(at), F32).reshape(n1, 1, n2), jnp.asarray(-np.sin(at), F32).reshape(n1, 1, n2))


def hyena_long_conv(x0, u, ff, fb, bias):
    n, W = u.shape
    n2 = HY_N2
    n1 = 2 * n // n2
    assert 2 * n == n1 * n2 and n1 % (2 * SUBLANES) == 0
    w_fwd, w_inv, w2r, w2i, twr, twi = _dft_tables(n1, n2)
    cols = n2 * W
    assert cols % HY_COLS == 0
    view = lambda a: a.reshape(n1 // 2, cols)
    in_spec = pl.BlockSpec((n1 // 2, HY_COLS), lambda j: (0, j))
    sp_spec = pl.BlockSpec((2 * n1, HY_COLS), lambda j: (0, j))
    spec_shape = jax.ShapeDtypeStruct((2 * n1, cols), BF16)
    au, af, ab = pl.pallas_call(
        _hy_outer_fwd_kernel,
        grid=(cols // HY_COLS,),
        in_specs=[pl.BlockSpec((2 * n1, n1 // 2), lambda j: (0, 0)), in_spec, in_spec, in_spec],
        out_specs=[sp_spec] * 3,
        out_shape=[spec_shape] * 3,
        compiler_params=_params("arbitrary"),
    )(w_fwd, view(u), view(ff), view(fb))
    k_spec = pl.BlockSpec((2, None, n2, W), lambda k: (0, k, 0, 0))
    tw_spec = pl.BlockSpec((None, 1, n2), lambda k: (k, 0, 0))
    w2_spec = pl.BlockSpec((n2, n2), lambda k: (0, 0))
    as4 = lambda a: a.reshape(2, n1, n2, W)
    c = pl.pallas_call(
        _hy_mid_kernel,
        grid=(n1,),
        in_specs=[w2_spec, w2_spec, tw_spec, tw_spec, k_spec, k_spec, k_spec],
        out_specs=k_spec,
        out_shape=jax.ShapeDtypeStruct((2, n1, n2, W), BF16),
        compiler_params=_params("arbitrary"),
    )(w2r, w2i, twr, twi, as4(au), as4(af), as4(ab))
    bias_t = jnp.tile(bias.astype(F32), HY_COLS // W)[None, :]
    out = pl.pallas_call(
        _hy_outer_inv_kernel,
        grid=(cols // HY_COLS,),
        in_specs=[pl.BlockSpec((n1 // 2, 2 * n1), lambda j: (0, 0)), sp_spec, in_spec, in_spec,
                  pl.BlockSpec((1, HY_COLS), lambda j: (0, 0))],
        out_specs=in_spec,
        out_shape=jax.ShapeDtypeStruct((n1 // 2, cols), BF16),
        compiler_params=_params("arbitrary"),
    )(w_inv, c.reshape(2 * n1, cols), view(x0), view(u), bias_t)
    return out.reshape(n, W)


def _hy_small_kernel(f_ref, finv_ref, x0_ref, u_ref, ff_ref, fb_ref, bias_ref, o_ref):
    f = f_ref[...]
    nf = f.shape[0] // 2
    u = u_ref[...]
    spec = lambda a: jnp.dot(f, a.astype(BF16), preferred_element_type=F32)
    su, sf, sb = spec(u), spec(ff_ref[...]), spec(fb_ref[...])
    ur, ui = su[:nf], su[nf:]
    hr, hi = sf[:nf] + sb[:nf], sf[nf:] - sb[nf:]
    p = jnp.concatenate([ur * hr - ui * hi, ur * hi + ui * hr], axis=0).astype(BF16)
    conv = jnp.dot(finv_ref[...], p, preferred_element_type=F32)
    o_ref[...] = (x0_ref[...] * (conv + u * bias_ref[...])).astype(o_ref.dtype)


def hyena_short_conv(x0, u, ff, fb, bias):
    n, W = u.shape
    ang = 2.0 * np.pi * np.arange(2 * n)[:, None] * np.arange(n)[None, :] / (2 * n)
    f = jnp.asarray(np.concatenate([np.cos(ang), -np.sin(ang)], axis=0), BF16)
    finv = jnp.asarray(np.concatenate([np.cos(ang).T, -np.sin(ang).T], axis=1) / (2 * n), BF16)
    args = [f, finv, x0, u, ff, fb, bias.astype(F32)[None, :]]
    return pl.pallas_call(
        _hy_small_kernel,
        grid=(1,),
        in_specs=[pl.BlockSpec(a.shape, lambda i: (0, 0)) for a in args],
        out_specs=pl.BlockSpec((n, W), lambda i: (0, 0)),
        out_shape=jax.ShapeDtypeStruct((n, W), BF16),
        compiler_params=_params("arbitrary"),
    )(*args)


def hyena(z, short_w, short_b, filt, bias):
    n = z.shape[0]
    x0, u = hyena_pre(z, short_w, short_b)
    ff, fb = hyena_filter(n, *filt)
    if 2 * n >= HY_MIN_N1 * HY_N2:
        return hyena_long_conv(x0, u, ff, fb, bias)
    return hyena_short_conv(x0, u, ff, fb, bias)


def _row_tile(m):
    return 1024 if m % 1024 == 0 else m


def _channel_mixer(h, combine, w_gate, w_up, w_down, j, n_experts):
    m = h.shape[0]
    act = expert_glu(h, w_gate, w_up, combine, j * n_experts, n_experts, tm=_row_tile(m), tn=256)
    k = w_down.shape[1]
    return matmul([act], w_down, j, tm=min(_row_tile(m), 4 * 1024 * 1024 // k), tn=512 if k <= 2048 else 256)


def kernel(x, c, ctx, c_ctx, ada_w, ada_b, norm_g, w_in, w_out, ret_decay, s5_a_re, s5_a_im, s5_log_dt, s5_b_re, s5_b_im, s5_c_re, s5_c_im, s5_d, s5_w_glu, hy_short_w, hy_short_b, hy_w1, hy_b1, hy_w2, hy_b2, hy_w3, hy_b3, hy_freq, hy_bias, ffn_w_gate, ffn_w_up, ffn_w_down, moe_w_router, moe_b_router, moe_w_gate, moe_w_up, moe_w_down):
    depth = ada_w.shape[0]
    xs = {0: x[0], 1: ctx[0]}
    n_lat, n_ctx = xs[0].shape[0], xs[1].shape[0]
    cond = jnp.concatenate([jax.nn.silu(c), jax.nn.silu(c_ctx)[None], jnp.zeros((SUBLANES - 2, D_MODEL), F32)], axis=0)
    mod = ada_modulation(cond.astype(BF16), ada_w, ada_b[:, None, :])
    mod = mod.reshape(depth, SUBLANES, N_MOD, D_MODEL).transpose(0, 2, 1, 3)
    rope = {0: rope_tables(n_lat),
            1: (jnp.ones((n_ctx, RET_DK), F32), jnp.zeros((n_ctx, RET_DK), F32))}
    ones = {r: jnp.ones((xs[r].shape[0], 1), F32) for r in xs}
    moe_gate = moe_w_gate.reshape((-1,) + moe_w_gate.shape[2:])
    moe_up = moe_w_up.reshape((-1,) + moe_w_up.shape[2:])
    moe_down = moe_w_down.reshape(moe_w_down.shape[0], -1, moe_w_down.shape[3])
    for l in range(depth):
        last = l == depth - 1
        g4, mod_l = norm_g[l], mod[l]
        log_gamma = jax.nn.log_sigmoid(ret_decay[l].astype(F32))
        tabs = s5_tables(s5_a_re[l], s5_a_im[l], s5_log_dt[l], s5_b_re[l], s5_b_im[l], s5_c_re[l], s5_c_im[l])
        filt = (hy_w1[l], hy_b1[l], hy_w2[l], hy_b2[l], hy_w3[l], hy_b3[l], hy_freq[l])
        j = l // 2
        if l % 2 == 1:
            router = (jnp.pad(moe_w_router[j].astype(F32), ((0, 0), (0, ROUTER_PAD - N_EXPERTS))),
                      jnp.concatenate([moe_b_router[j].astype(F32), jnp.full((ROUTER_PAD - N_EXPERTS,), NEG_BIG, F32)])[None, :])
        else:
            router = None
        ret_state = jnp.zeros((2, RET_HEADS, RET_DK, RET_DV), F32)
        s5_state = jnp.zeros((2, SUBLANES, S5_LANES), F32)
        for r in (1, 0):
            xr = xs[r]
            m = xr.shape[0]
            h = prenorm(xr, g4, mod_l, r)
            z = matmul([h], w_in, l, tm=_row_tile(m), tn=512)
            y_ret, ret_fin = retention(z, log_gamma, rope[r][0], rope[r][1], ret_state)
            y_s5, s5_fin = s5(z, tabs, s5_state, s5_d[l][None, :], s5_w_glu, l)
            if r == 1:
                ret_state, s5_state = ret_fin, s5_fin
                if last:
                    continue
            y_hy = hyena(z, hy_short_w[l], hy_short_b[l], filt, hy_bias[l])
            y = matmul([y_ret, y_s5, y_hy], w_out, l, tm=_row_tile(m), tn=512)
            if router is None:
                xr, h = resid_update(xr, y, g4, mod_l, r, 0, has_next=True)
                f = _channel_mixer(h, ones[r], ffn_w_gate, ffn_w_up, ffn_w_down, j, 1)
            else:
                xr, h, combine = resid_update(xr, y, g4, mod_l, r, 0, has_next=True, router=router)
                f = _channel_mixer(h, combine, moe_gate, moe_up, moe_down, j, N_EXPERTS)
            xs[r] = resid_update(xr, f, g4, mod_l, r, 1, has_next=False)[0]
    return xs[0][None]
```

```python
import functools
import math

import jax
import jax.numpy as jnp
import numpy as np
from jax import lax
from jax.experimental import pallas as pl
from jax.experimental.pallas import tpu as pltpu

D_MODEL = 4096
GRID_W = 64
N_MOD = 6
EPS = 1e-6

RET_HEADS = 8
RET_DK = 128
RET_DV = 256
RET_QK = RET_HEADS * RET_DK
RET_V = RET_HEADS * RET_DV
RET_CHUNK = 128
ROPE_BASE = 10000.0

S5_WIDTH = 1024
S5_GROUP = 16
S5_GROUPS = S5_WIDTH // S5_GROUP
S5_STATE = 64
S5_MAX_RE = -1e-4
S5_LANES = S5_GROUPS * S5_STATE
S5_BLK_GROUPS = 8
S5_BLK = S5_BLK_GROUPS * S5_STATE
S5_NBLK = S5_GROUPS // S5_BLK_GROUPS

HY_WIDTH = 1024
HY_ORDER = 2
HY_EMB = 33
HY_EMB_PAD = 64
HY_BANDS = (HY_EMB - 1) // 2
HY_FAST_DECAY = 0.3
HY_SLOW_DECAY = 1.5
HY_TARGET = 1e-2
HY_N2 = 256
HY_MIN_N1 = 16

COL_Q, COL_K, COL_V, COL_G = 0, RET_QK, 2 * RET_QK, 2 * RET_QK + RET_V
COL_U = 2 * RET_QK + 2 * RET_V
COL_Z = COL_U + S5_WIDTH

N_EXPERTS = 8
ROUTER_PAD = 128
NEG_BIG = -1e30

BF16 = jnp.bfloat16
F32 = jnp.float32
HIGHEST = lax.Precision.HIGHEST

VMEM_LIMIT_BYTES = 56 * 1024 * 1024
CAST_ROWS = 256
SUBLANES = 8
LANES = 128


def _params(*sem):
    return pltpu.CompilerParams(dimension_semantics=sem, vmem_limit_bytes=VMEM_LIMIT_BYTES)


def _cast_weight(w_ref, wb_ref):
    k = w_ref.shape[0]
    rows = min(CAST_ROWS, k)

    def body(c, carry):
        r = pl.multiple_of(c * rows, rows)
        wb_ref[pl.ds(r, rows), :] = w_ref[pl.ds(r, rows), :].astype(BF16)
        return carry

    lax.fori_loop(0, k // rows, body, 0)


def _mm_kernel(*refs, widths):
    a_refs = refs[:len(widths)]
    w_ref, o_ref, wb_ref = refs[len(widths):]

    @pl.when(pl.program_id(1) == 0)
    def _():
        _cast_weight(w_ref, wb_ref)

    acc, off = None, 0
    for a_ref, wd in zip(a_refs, widths):
        part = jnp.dot(a_ref[...], wb_ref[off:off + wd, :], preferred_element_type=F32)
        acc = part if acc is None else acc + part
        off += wd
    o_ref[...] = acc.astype(o_ref.dtype)


def matmul(a_parts, w, layer, *, tm, tn, out_dtype=F32):
    a_parts = list(a_parts)
    m = a_parts[0].shape[0]
    widths = tuple(a.shape[1] for a in a_parts)
    _, k, n = w.shape
    assert sum(widths) == k and m % tm == 0 and n % tn == 0, (widths, w.shape, tm, tn)
    return pl.pallas_call(
        functools.partial(_mm_kernel, widths=widths),
        grid=(n // tn, m // tm),
        in_specs=[pl.BlockSpec((tm, wd), lambda j, i: (i, 0)) for wd in widths]
                 + [pl.BlockSpec((None, k, tn), lambda j, i: (layer, 0, j))],
        out_specs=pl.BlockSpec((tm, tn), lambda j, i: (i, j)),
        out_shape=jax.ShapeDtypeStruct((m, n), out_dtype),
        scratch_shapes=[pltpu.VMEM((k, tn), BF16)],
        compiler_params=_params("arbitrary", "arbitrary"),
    )(*a_parts, w)


def _glu_kernel(a_ref, wg_ref, wu_ref, c_ref, o_ref, wgb_ref, wub_ref):
    @pl.when(pl.program_id(2) == 0)
    def _():
        _cast_weight(wg_ref, wgb_ref)
        _cast_weight(wu_ref, wub_ref)

    a = a_ref[...]
    g = jnp.dot(a, wgb_ref[...], preferred_element_type=F32)
    u = jnp.dot(a, wub_ref[...], preferred_element_type=F32)
    act = g * jax.nn.sigmoid(g) * u
    e = pl.program_id(0)
    c = c_ref[...]
    lane = lax.broadcasted_iota(jnp.int32, c.shape, 1)
    ce = jnp.sum(jnp.where(lane == e, c, 0.0), axis=1, keepdims=True)
    o_ref[...] = (act * ce).astype(o_ref.dtype)


def expert_glu(a, w_gate, w_up, combine, e0, e, *, tm, tn):
    m, k = a.shape
    _, k2, f = w_gate.shape
    assert k == k2 and m % tm == 0 and f % tn == 0
    nj = f // tn
    return pl.pallas_call(
        _glu_kernel,
        grid=(e, nj, m // tm),
        in_specs=[pl.BlockSpec((tm, k), lambda e_, j, i: (i, 0)),
                  pl.BlockSpec((None, k, tn), lambda e_, j, i: (e0 + e_, 0, j)),
                  pl.BlockSpec((None, k, tn), lambda e_, j, i: (e0 + e_, 0, j)),
                  pl.BlockSpec((tm, combine.shape[1]), lambda e_, j, i: (i, 0))],
        out_specs=pl.BlockSpec((tm, tn), lambda e_, j, i: (i, e_ * nj + j)),
        out_shape=jax.ShapeDtypeStruct((m, e * f), BF16),
        scratch_shapes=[pltpu.VMEM((k, tn), BF16), pltpu.VMEM((k, tn), BF16)],
        compiler_params=_params("arbitrary", "arbitrary", "arbitrary"),
    )(a, w_gate, w_up, combine)


def _mm_acc_kernel(a_ref, w_ref, o_ref):
    part = jnp.dot(a_ref[...], w_ref[...].astype(BF16), preferred_element_type=F32)

    @pl.when(pl.program_id(2) == 0)
    def _():
        o_ref[...] = part

    @pl.when(pl.program_id(2) > 0)
    def _():
        o_ref[...] += part


def matmul_kslabs(a, w, layer, kb, *, tm, tn):
    m, k = a.shape
    _, k2, n = w.shape
    assert k == k2 and m % tm == 0 and n % tn == 0 and k % kb == 0
    return pl.pallas_call(
        _mm_acc_kernel,
        grid=(n // tn, m // tm, k // kb),
        in_specs=[pl.BlockSpec((tm, kb), lambda j, i, e: (i, e)),
                  pl.BlockSpec((None, kb, tn), lambda j, i, e: (layer, e, j))],
        out_specs=pl.BlockSpec((tm, tn), lambda j, i, e: (i, j)),
        out_shape=jax.ShapeDtypeStruct((m, n), F32),
        compiler_params=_params("arbitrary", "arbitrary", "arbitrary"),
    )(a, w)


def _ada_kernel(a_ref, w_ref, b_ref, o_ref):
    o_ref[...] = jnp.dot(a_ref[...], w_ref[...].astype(BF16), preferred_element_type=F32) + b_ref[...]


def ada_modulation(cond, ada_w, ada_b, *, tn=1024):
    depth, d, n = ada_w.shape
    return pl.pallas_call(
        _ada_kernel,
        grid=(depth, n // tn),
        in_specs=[pl.BlockSpec((SUBLANES, d), lambda l, j: (0, 0)),
                  pl.BlockSpec((None, d, tn), lambda l, j: (l, 0, j)),
                  pl.BlockSpec((None, 1, tn), lambda l, j: (l, 0, j))],
        out_specs=pl.BlockSpec((None, SUBLANES, tn), lambda l, j: (l, 0, j)),
        out_shape=jax.ShapeDtypeStruct((depth, SUBLANES, n), F32),
        compiler_params=_params("arbitrary", "arbitrary"),
    )(cond, ada_w, ada_b)


NORM_ROWS = 256


def _rms(x):
    return x * lax.rsqrt(jnp.mean(x * x, axis=-1, keepdims=True) + EPS)


def _modulated(x, g, mod_ref, row, shift_idx):
    shift = mod_ref[shift_idx, row:row + 1, :]
    scale = mod_ref[shift_idx + 1, row:row + 1, :]
    return _rms(x) * g * (1.0 + scale) + shift


def _prenorm_kernel(x_ref, g_ref, mod_ref, h_ref, *, row):
    h_ref[...] = _modulated(x_ref[...], g_ref[0:1, :], mod_ref, row, 0).astype(BF16)


def prenorm(x, g4, mod_l, row):
    m, d = x.shape
    tr = min(NORM_ROWS, m)
    return pl.pallas_call(
        functools.partial(_prenorm_kernel, row=row),
        grid=(m // tr,),
        in_specs=[pl.BlockSpec((tr, d), lambda i: (i, 0)),
                  pl.BlockSpec((4, d), lambda i: (0, 0)),
                  pl.BlockSpec((N_MOD, SUBLANES, d), lambda i: (0, 0, 0))],
        out_specs=pl.BlockSpec((tr, d), lambda i: (i, 0)),
        out_shape=jax.ShapeDtypeStruct((m, d), BF16),
        compiler_params=_params("arbitrary"),
    )(x, g4, mod_l)


def _top2_combine(logits):
    lane = lax.broadcasted_iota(jnp.int32, logits.shape, 1)
    big = jnp.int32(logits.shape[1])
    v1 = jnp.max(logits, axis=1, keepdims=True)
    i1 = jnp.min(jnp.where(logits == v1, lane, big), axis=1, keepdims=True)
    rest = jnp.where(lane == i1, NEG_BIG, logits)
    v2 = jnp.max(rest, axis=1, keepdims=True)
    i2 = jnp.min(jnp.where(rest == v2, lane, big), axis=1, keepdims=True)
    e2 = jnp.exp(v2 - v1)
    w1 = 1.0 / (1.0 + e2)
    w2 = e2 * w1
    return jnp.where(lane == i1, w1, 0.0) + jnp.where(lane == i2, w2, 0.0)


def _resid_kernel(*refs, row, stage, has_next, has_router):
    x_ref, y_ref, g_ref, mod_ref = refs[:4]
    pos = 4
    if has_next and stage == 1:
        gn_ref, modn_ref = refs[pos:pos + 2]
        pos += 2
    if has_router:
        wr_ref, br_ref = refs[pos:pos + 2]
        pos += 2
    xo_ref = refs[pos]
    pos += 1
    gate = mod_ref[3 * stage + 2, row:row + 1, :]
    xn = x_ref[...] + gate * (_rms(y_ref[...]) * g_ref[2 * stage + 1:2 * stage + 2, :])
    xo_ref[...] = xn
    if has_next:
        h_ref = refs[pos]
        pos += 1
        if stage == 0:
            h = _modulated(xn, g_ref[2:3, :], mod_ref, row, 3)
        else:
            h = _modulated(xn, gn_ref[0:1, :], modn_ref, row, 0)
        h_ref[...] = h.astype(BF16)
        if has_router:
            c_ref = refs[pos]
            logits = jnp.dot(h, wr_ref[...], preferred_element_type=F32, precision=HIGHEST) + br_ref[...]
            c_ref[...] = _top2_combine(logits)


def resid_update(x, y, g4, mod_l, row, stage, *, has_next, router=None, nxt=None):
    m, d = x.shape
    tr = min(NORM_ROWS, m)
    row_spec = pl.BlockSpec((tr, d), lambda i: (i, 0))
    in_specs = [row_spec, row_spec, pl.BlockSpec((4, d), lambda i: (0, 0)),
                pl.BlockSpec((N_MOD, SUBLANES, d), lambda i: (0, 0, 0))]
    args = [x, y, g4, mod_l]
    if has_next and stage == 1:
        in_specs += in_specs[2:4]
        args += list(nxt)
    out_specs = [row_spec]
    out_shape = [jax.ShapeDtypeStruct((m, d), F32)]
    if router is not None:
        in_specs += [pl.BlockSpec((d, ROUTER_PAD), lambda i: (0, 0)), pl.BlockSpec((1, ROUTER_PAD), lambda i: (0, 0))]
        args += list(router)
    if has_next:
        out_specs.append(row_spec)
        out_shape.append(jax.ShapeDtypeStruct((m, d), BF16))
        if router is not None:
            out_specs.append(pl.BlockSpec((tr, ROUTER_PAD), lambda i: (i, 0)))
            out_shape.append(jax.ShapeDtypeStruct((m, ROUTER_PAD), F32))
    return pl.pallas_call(
        functools.partial(_resid_kernel, row=row, stage=stage, has_next=has_next, has_router=router is not None),
        grid=(m // tr,),
        in_specs=in_specs,
        out_specs=out_specs,
        out_shape=out_shape,
        compiler_params=_params("arbitrary"),
    )(*args)


def _rope(x, cos, sin):
    return x * cos + pltpu.roll(x, RET_DK // 2, axis=1) * sin


def _ret_kernel(*refs, nsteps, nchunks, reverse):
    if reverse:
        lg_ref, q_ref, k_ref, v_ref, cos_ref, sin_ref, r0_ref, o_ref, rfin_ref, r_ref = refs
    else:
        lg_ref, q_ref, k_ref, v_ref, cos_ref, sin_ref, r0_ref, g_ref, ob_ref, o_ref, rfin_ref, r_ref = refs
    s = pl.program_id(0)
    C = RET_CHUNK

    @pl.when(s == 0)
    def _():
        r_ref[...] = r0_ref[...]

    n_i = lax.broadcasted_iota(jnp.int32, (C, C), 0)
    m_i = lax.broadcasted_iota(jnp.int32, (C, C), 1)
    dist = ((m_i - n_i) if reverse else (n_i - m_i)).astype(F32)
    valid = (dist > 0) if reverse else (dist >= 0)
    pos = lax.broadcasted_iota(jnp.int32, (C, 1), 0).astype(F32)
    to_state = (pos if reverse else (C - 1 - pos))
    from_state = ((C - pos) if reverse else (pos + 1))
    scale = RET_DK ** -0.5

    for h in range(RET_HEADS):
        lg = lg_ref[1 if reverse else 0, h]
        decay = jnp.where(valid, jnp.exp(lg * jnp.maximum(dist, 0.0)), 0.0)
        xi = jnp.exp(lg * from_state)
        zeta = jnp.exp(lg * to_state)
        chunk_decay = jnp.exp(lg * C)
        qc = slice(h * RET_DK, (h + 1) * RET_DK)
        vc = slice(h * RET_DV, (h + 1) * RET_DV)
        for j in (range(nchunks - 1, -1, -1) if reverse else range(nchunks)):
            rows = slice(j * C, (j + 1) * C)
            cos, sin = cos_ref[rows, :], sin_ref[rows, :]
            q = _rope(q_ref[rows, qc], cos, sin)
            k = _rope(k_ref[rows, qc], cos, sin) * scale
            vb = v_ref[rows, vc].astype(BF16)
            qb = q.astype(BF16)
            scores = lax.dot_general(qb, k.astype(BF16), (((1,), (1,)), ((), ())), preferred_element_type=F32)
            inner = jnp.dot((scores * decay).astype(BF16), vb, preferred_element_type=F32)
            state = r_ref[h]
            o = inner + jnp.dot(qb, state.astype(BF16), preferred_element_type=F32) * xi
            kv = lax.dot_general((k * zeta).astype(BF16), vb, (((0,), (0,)), ((), ())), preferred_element_type=F32)
            r_ref[h] = chunk_decay * state + kv
            if reverse:
                o_ref[rows, vc] = o.astype(o_ref.dtype)
            else:
                t = o + ob_ref[rows, vc].astype(F32)
                t = t * lax.rsqrt(jnp.mean(t * t, axis=-1, keepdims=True) + EPS)
                g = g_ref[rows, vc]
                o_ref[rows, vc] = (t * (g * jax.nn.sigmoid(g))).astype(o_ref.dtype)

    @pl.when(s == nsteps - 1)
    def _():
        rfin_ref[...] = r_ref[...]


def retention(z, log_gamma, rope_cos, rope_sin, r0):
    L = z.shape[0]
    nchunks = min(4, L // RET_CHUNK)
    rows = nchunks * RET_CHUNK
    nsteps = L // rows

    def call(reverse, extra, extra_specs):
        d = 1 if reverse else 0
        rmap = (lambda c: (lambda s: (nsteps - 1 - s, c))) if reverse else (lambda c: (lambda s: (s, c)))
        st_spec = pl.BlockSpec((None, RET_HEADS, RET_DK, RET_DV), lambda s: (d, 0, 0, 0))
        return pl.pallas_call(
            functools.partial(_ret_kernel, nsteps=nsteps, nchunks=nchunks, reverse=reverse),
            grid=(nsteps,),
            in_specs=[pl.BlockSpec(memory_space=pltpu.SMEM),
                      pl.BlockSpec((rows, RET_QK), rmap(COL_Q // RET_QK)),
                      pl.BlockSpec((rows, RET_QK), rmap(COL_K // RET_QK)),
                      pl.BlockSpec((rows, RET_V), rmap(COL_V // RET_V)),
                      pl.BlockSpec((rows, RET_DK), rmap(0)), pl.BlockSpec((rows, RET_DK), rmap(0)),
                      st_spec] + [pl.BlockSpec((rows, RET_V), rmap(c)) for c in extra_specs],
            out_specs=[pl.BlockSpec((rows, RET_V), rmap(0)),
                       pl.BlockSpec((RET_HEADS, RET_DK, RET_DV), lambda s: (0, 0, 0))],
            out_shape=[jax.ShapeDtypeStruct((L, RET_V), BF16),
                       jax.ShapeDtypeStruct((RET_HEADS, RET_DK, RET_DV), F32)],
            scratch_shapes=[pltpu.VMEM((RET_HEADS, RET_DK, RET_DV), F32)],
            compiler_params=_params("arbitrary"),
        )(log_gamma, z, z, z, rope_cos, rope_sin, r0, *extra)

    o_b, rfin_b = call(True, [], [])
    o, rfin_f = call(False, [z, o_b], [COL_G // RET_V, 0])
    return o, jnp.stack([rfin_f, rfin_b])


def rope_tables(n_tokens):
    rows = n_tokens // GRID_W
    row = jnp.repeat(jnp.arange(rows, dtype=F32), GRID_W)
    col = jnp.tile(jnp.arange(GRID_W, dtype=F32), rows)
    n_freq = RET_DK // 4
    inv = ROPE_BASE ** (-jnp.arange(n_freq, dtype=F32) / n_freq)
    ang = jnp.concatenate([row[:, None] * inv, col[:, None] * inv], axis=-1)
    cos, sin = jnp.cos(ang), jnp.sin(ang)
    return jnp.concatenate([cos, cos], axis=-1), jnp.concatenate([-sin, sin], axis=-1)


S5_ROWS = 256


def _s5_scan_block(x_re, x_im, pw, c_re, c_im, reverse):
    for d, (m_re, m_im) in zip((1, 2, 4), pw["steps"]):
        shift = (SUBLANES - d) if reverse else d
        s_re = pltpu.roll(x_re, shift, axis=0)
        s_im = pltpu.roll(x_im, shift, axis=0)
        x_re, x_im = x_re + (m_re * s_re - m_im * s_im), x_im + (m_re * s_im + m_im * s_re)
    e_re, e_im = pw["edge"]
    x_re = x_re + (e_re * c_re - e_im * c_im)
    x_im = x_im + (e_re * c_im + e_im * c_re)
    return x_re, x_im


def _s5_kernel(*refs, nsteps, reverse):
    if reverse:
        u_ref, bmat_ref, cmat_ref, pow_ref, x0_ref, y_out_ref, xfin_ref, bu_ref, carry_ref = refs
    else:
        (u_ref, bmat_ref, cmat_ref, pow_ref, x0_ref, d_ref, wglu_ref, yb_ref,
         out_ref, xfin_ref, bu_ref, carry_ref, y_ref, wb_ref) = refs
    s = pl.program_id(0)
    rows = u_ref.shape[0]
    nb = rows // SUBLANES
    W = S5_BLK

    @pl.when(s == 0)
    def _():
        carry_ref[...] = x0_ref[...]
        if not reverse:
            _cast_weight(wglu_ref, wb_ref)

    u = u_ref[...]
    ub = u.astype(BF16)
    y_dst = y_out_ref if reverse else y_ref
    for j in range(S5_NBLK):
        bu_ref[...] = jnp.dot(ub[:, j * LANES:(j + 1) * LANES], bmat_ref[j], preferred_element_type=F32)
        lo = j * W
        pw = {
            "steps": [(pow_ref[2 * t, :, lo:lo + W], pow_ref[2 * t + 1, :, lo:lo + W]) for t in range(3)],
            "edge": (pow_ref[6, :, lo:lo + W], pow_ref[7, :, lo:lo + W]),
        }
        c0 = (carry_ref[0:1, lo:lo + W], carry_ref[1:2, lo:lo + W])

        def body(b, c, pw=pw):
            bi = (nb - 1 - b) if reverse else b
            r = pl.multiple_of(bi * SUBLANES, SUBLANES)
            x_re, x_im = _s5_scan_block(bu_ref[pl.ds(r, SUBLANES), 0:W], bu_ref[pl.ds(r, SUBLANES), W:2 * W],
                                        pw, c[0], c[1], reverse)
            bu_ref[pl.ds(r, SUBLANES), 0:W] = x_re
            bu_ref[pl.ds(r, SUBLANES), W:2 * W] = x_im
            e = 0 if reverse else SUBLANES - 1
            return x_re[e:e + 1, :], x_im[e:e + 1, :]

        c_re, c_im = lax.fori_loop(0, nb, body, c0)
        carry_ref[0:1, lo:lo + W] = c_re
        carry_ref[1:2, lo:lo + W] = c_im
        y_dst[:, j * LANES:(j + 1) * LANES] = jnp.dot(bu_ref[...].astype(BF16), cmat_ref[j],
                                                      preferred_element_type=F32)

    if not reverse:
        y = y_ref[...] + yb_ref[...] + d_ref[...] * u
        z = jnp.dot(jax.nn.gelu(y).astype(BF16), wb_ref[...], preferred_element_type=F32)
        out_ref[...] = (z[:, :S5_WIDTH] * jax.nn.sigmoid(z[:, S5_WIDTH:])).astype(out_ref.dtype)

    @pl.when(s == nsteps - 1)
    def _():
        xfin_ref[...] = carry_ref[...]


def s5_tables(a_re, a_im, log_dt, b_re, b_im, c_re, c_im):
    lam = lax.complex(jnp.minimum(a_re.astype(F32), S5_MAX_RE), a_im.astype(F32))
    lam_bar = jnp.exp(lam * jnp.exp(log_dt.astype(F32))[..., None])
    b_mat = lax.complex(b_re.astype(F32), b_im.astype(F32))
    b_bar = ((lam_bar - 1.0) / lam)[..., None] * b_mat[None]
    eye = jnp.eye(S5_BLK_GROUPS, dtype=F32)

    def in_map(m):
        m = m.reshape(S5_NBLK, S5_BLK_GROUPS, S5_STATE, S5_GROUP)
        return jnp.einsum('jgph,gk->jghkp', m, eye).reshape(S5_NBLK, LANES, S5_BLK)

    bmat = jnp.stack([jnp.concatenate([in_map(jnp.real(b_bar[d])), in_map(jnp.imag(b_bar[d]))], axis=-1)
                      for d in range(2)]).astype(BF16)

    def out_map(m):
        m = m.reshape(S5_NBLK, S5_BLK_GROUPS, S5_GROUP, S5_STATE)
        return jnp.einsum('jghp,gk->jgpkh', m, eye).reshape(S5_NBLK, S5_BLK, LANES)

    cmat = jnp.concatenate([out_map(c_re.astype(F32)), out_map(-c_im.astype(F32))], axis=1).astype(BF16)

    lam_flat = lam_bar.reshape(2, 1, S5_LANES)
    r = jnp.arange(SUBLANES)[None, :, None]

    def masked_pow(d, reverse):
        ok = (r + d < SUBLANES) if reverse else (r >= d)
        return jnp.where(ok, lam_flat[1 if reverse else 0] ** d, 0.0)

    def tables(reverse):
        rows = []
        for d in (1, 2, 4):
            m = masked_pow(d, reverse)
            rows += [jnp.real(m), jnp.imag(m)]
        dist = (SUBLANES - r) if reverse else (r + 1)
        e = lam_flat[1 if reverse else 0] ** dist
        rows += [jnp.real(e), jnp.imag(e)]
        return jnp.stack([jnp.broadcast_to(t, (1, SUBLANES, S5_LANES))[0] for t in rows])

    pows = jnp.stack([tables(False), tables(True)]).astype(F32)
    return bmat, cmat, pows


def s5(z, tables, x0, d_skip, w_glu, layer):
    bmat, cmat, pows = tables
    L = z.shape[0]
    rows = min(S5_ROWS, L)
    nsteps = L // rows
    ucol = COL_U // S5_WIDTH

    def call(reverse, extra_in, extra_specs, out_dtype, scratch):
        d = 1 if reverse else 0
        rmap = (lambda s: (nsteps - 1 - s, 0)) if reverse else (lambda s: (s, 0))
        umap = (lambda s: (nsteps - 1 - s, ucol)) if reverse else (lambda s: (s, ucol))
        st_spec = pl.BlockSpec((None, SUBLANES, S5_LANES), lambda s: (d, 0, 0))
        return pl.pallas_call(
            functools.partial(_s5_kernel, nsteps=nsteps, reverse=reverse),
            grid=(nsteps,),
            in_specs=[pl.BlockSpec((rows, S5_WIDTH), umap),
                      pl.BlockSpec((None, S5_NBLK, LANES, 2 * S5_BLK), lambda s: (d, 0, 0, 0)),
                      pl.BlockSpec((S5_NBLK, 2 * S5_BLK, LANES), lambda s: (0, 0, 0)),
                      pl.BlockSpec((None, 8, SUBLANES, S5_LANES), lambda s: (d, 0, 0, 0)),
                      st_spec] + [sp(rmap) for sp in extra_specs],
            out_specs=[pl.BlockSpec((rows, S5_WIDTH), rmap), pl.BlockSpec((SUBLANES, S5_LANES), lambda s: (0, 0))],
            out_shape=[jax.ShapeDtypeStruct((L, S5_WIDTH), out_dtype),
                       jax.ShapeDtypeStruct((SUBLANES, S5_LANES), F32)],
            scratch_shapes=[pltpu.VMEM((rows, 2 * S5_BLK), F32), pltpu.VMEM((SUBLANES, S5_LANES), F32)] + scratch,
            compiler_params=_params("arbitrary"),
        )(z, bmat, cmat, pows, x0, *extra_in)

    yb, xfin_b = call(True, [], [], F32, [])
    out, xfin_f = call(
        False, [d_skip, w_glu, yb],
        [lambda rmap: pl.BlockSpec((1, S5_WIDTH), lambda s: (0, 0)),
         lambda rmap: pl.BlockSpec((None, S5_WIDTH, 2 * S5_WIDTH), lambda s: (layer, 0, 0)),
         lambda rmap: pl.BlockSpec((rows, S5_WIDTH), rmap)],
        BF16,
        [pltpu.VMEM((rows, S5_WIDTH), F32), pltpu.VMEM((S5_WIDTH, 2 * S5_WIDTH), BF16)])
    return out, jnp.stack([xfin_f, xfin_b])


HY_ROWS = 512
HY_COLS = 8192


def _hy_pre_kernel(*refs, nsteps):
    mains, prevs, nexts = refs[0:3], refs[3:6], refs[6:9]
    sws, sbs = refs[9:12], refs[12:15]
    x0_ref, u_ref = refs[15:17]
    i = pl.program_id(0)
    tr = mains[0].shape[0]
    row = lax.broadcasted_iota(jnp.int32, (tr, 1), 0)
    parts = []
    for c in range(HY_ORDER + 1):
        z = mains[c][...]
        before = jnp.where(i == 0, 0.0, prevs[c][SUBLANES - 1:SUBLANES, :])
        after = jnp.where(i == nsteps - 1, 0.0, nexts[c][0:1, :])
        down = jnp.where(row == 0, before, pltpu.roll(z, 1, axis=0))
        up = jnp.where(row == tr - 1, after, pltpu.roll(z, tr - 1, axis=0))
        sw = sws[c]
        parts.append(sw[0:1, :] * down + sw[1:2, :] * z + sw[2:3, :] * up + sbs[c][...])
    x0_ref[...] = parts[0].astype(x0_ref.dtype)
    u_ref[...] = (parts[1] * parts[2]).astype(u_ref.dtype)


def hyena_pre(z, short_w, short_b):
    L = z.shape[0]
    tr = min(HY_ROWS, L)
    nsteps = L // tr
    per = tr // SUBLANES
    last8 = L // SUBLANES - 1
    cb = COL_Z // HY_WIDTH
    specs, args = [], []
    for kind in range(3):
        for c in range(HY_ORDER + 1):
            if kind == 0:
                specs.append(pl.BlockSpec((tr, HY_WIDTH), lambda i, c=c: (i, cb + c)))
            elif kind == 1:
                specs.append(pl.BlockSpec((SUBLANES, HY_WIDTH), lambda i, c=c: (jnp.maximum(i * per - 1, 0), cb + c)))
            else:
                specs.append(pl.BlockSpec((SUBLANES, HY_WIDTH), lambda i, c=c: (jnp.minimum((i + 1) * per, last8), cb + c)))
            args.append(z)
    for c in range(HY_ORDER + 1):
        specs.append(pl.BlockSpec((3, HY_WIDTH), lambda i, c=c: (0, c)))
        args.append(short_w)
    sb = short_b.reshape(1, -1)
    for c in range(HY_ORDER + 1):
        specs.append(pl.BlockSpec((1, HY_WIDTH), lambda i, c=c: (0, c)))
        args.append(sb)
    out_spec = pl.BlockSpec((tr, HY_WIDTH), lambda i: (i, 0))
    return pl.pallas_call(
        functools.partial(_hy_pre_kernel, nsteps=nsteps),
        grid=(nsteps,),
        in_specs=specs,
        out_specs=[out_spec, out_spec],
        out_shape=[jax.ShapeDtypeStruct((L, HY_WIDTH), BF16)] * 2,
        compiler_params=_params("arbitrary"),
    )(*args)


def _hy_filter_kernel(z_ref, w1_ref, b1_ref, w2_ref, b2_ref, w3_ref, b3_ref, fr_ref, dl_ref, ff_ref, fb_ref):
    i = pl.program_id(0)
    z = z_ref[...]
    tr = z.shape[0]
    h = jnp.sin(fr_ref[0:1, :] * (jnp.dot(z, w1_ref[...], preferred_element_type=F32, precision=HIGHEST) + b1_ref[...]))
    h = jnp.sin(fr_ref[1:2, :] * (jnp.dot(h, w2_ref[...], preferred_element_type=F32, precision=HIGHEST) + b2_ref[...]))
    h = jnp.dot(h, w3_ref[...], preferred_element_type=F32, precision=HIGHEST) + b3_ref[...]
    decay = jnp.exp(-z[:, 0:1] * dl_ref[...])
    ff_ref[...] = (h[:, :HY_WIDTH] * decay).astype(ff_ref.dtype)
    row = i * tr + lax.broadcasted_iota(jnp.int32, (tr, 1), 0)
    fb_ref[...] = jnp.where(row == 0, 0.0, h[:, HY_WIDTH:] * decay).astype(fb_ref.dtype)


def hyena_features(n):
    t = jnp.linspace(0.0, 1.0, n, dtype=F32)[:, None]
    w = 2.0 * math.pi * jnp.arange(n, dtype=F32)[:, None] / n
    f = jnp.linspace(1e-4, HY_BANDS - 1, HY_BANDS, dtype=F32)[None, :]
    z = jnp.concatenate([t, jnp.cos(f * w), -jnp.sin(f * w)], axis=-1)
    return jnp.pad(z, ((0, 0), (0, HY_EMB_PAD - HY_EMB)))


def hyena_filter(n, w1, b1, w2, b2, w3, b3, freq):
    tr = min(HY_ROWS, n)
    max_decay = math.log(HY_TARGET) / HY_FAST_DECAY
    min_decay = math.log(HY_TARGET) / HY_SLOW_DECAY
    deltas = jnp.abs(jnp.linspace(min_decay, max_decay, HY_WIDTH, dtype=F32))[None, :]
    w1p = jnp.pad(w1.astype(F32), ((0, HY_EMB_PAD - HY_EMB), (0, 0)))
    full = lambda a: pl.BlockSpec(a.shape, lambda i: (0,) * a.ndim)
    ins = [hyena_features(n), w1p, b1.reshape(1, -1), w2, b2.reshape(1, -1), w3, b3.reshape(1, -1), freq, deltas]
    out_spec = pl.BlockSpec((tr, HY_WIDTH), lambda i: (i, 0))
    return pl.pallas_call(
        _hy_filter_kernel,
        grid=(n // tr,),
        in_specs=[pl.BlockSpec((tr, HY_EMB_PAD), lambda i: (i, 0))] + [full(a) for a in ins[1:]],
        out_specs=[out_spec, out_spec],
        out_shape=[jax.ShapeDtypeStruct((n, HY_WIDTH), BF16)] * 2,
        compiler_params=_params("arbitrary"),
    )(*ins)


def _hy_outer_fwd_kernel(w_ref, u_ref, f_ref, b_ref, au_ref, af_ref, ab_ref):
    w = w_ref[...]
    for src, dst in ((u_ref, au_ref), (f_ref, af_ref), (b_ref, ab_ref)):
        dst[...] = jnp.dot(w, src[...], preferred_element_type=F32).astype(BF16)


def _hy_mid_kernel(w2r_ref, w2i_ref, twr_ref, twi_ref, au_ref, af_ref, ab_ref, c_ref):
    n2 = HY_N2
    twr, twi = twr_ref[...], twi_ref[...]
    w2r, w2i = w2r_ref[...], w2i_ref[...]
    mr = w2r * twr - w2i * twi
    mi = w2r * twi + w2i * twr
    mblk = jnp.concatenate([jnp.concatenate([mr, -mi], axis=1), jnp.concatenate([mi, mr], axis=1)], axis=0).astype(BF16)
    x = jnp.concatenate([r[...].reshape(2 * n2, HY_WIDTH) for r in (au_ref, af_ref, ab_ref)], axis=1)
    s = jnp.dot(mblk, x, preferred_element_type=F32)
    W = HY_WIDTH
    ur, ui = s[:n2, 0:W], s[n2:, 0:W]
    hr = s[:n2, W:2 * W] + s[:n2, 2 * W:]
    hi = s[n2:, W:2 * W] - s[n2:, 2 * W:]
    p = jnp.concatenate([ur * hr - ui * hi, ur * hi + ui * hr], axis=0).astype(BF16)
    c = lax.dot_general(mblk, p, (((0,), (0,)), ((), ())), preferred_element_type=F32)
    c_ref[...] = c.astype(BF16).reshape(2, n2, W)


def _hy_outer_inv_kernel(w_ref, c_ref, x0_ref, u_ref, bias_ref, o_ref):
    conv = jnp.dot(w_ref[...], c_ref[...], preferred_element_type=F32)
    o_ref[...] = (x0_ref[...].astype(F32) * (conv + u_ref[...].astype(F32) * bias_ref[...])).astype(o_ref.dtype)


def _dft_tables(n1, n2):
    n = n1 * n2
    k1 = np.arange(n1)[:, None]
    t1 = np.arange(n1 // 2)[None, :]
    a1 = 2.0 * np.pi * k1 * t1 / n1
    w_fwd = np.concatenate([np.cos(a1), -np.sin(a1)], axis=0)
    w_inv = np.concatenate([np.cos(a1).T, -np.sin(a1).T], axis=1) / n
    a2 = 2.0 * np.pi * np.arange(n2)[:, None] * np.arange(n2)[None, :] / n2
    at = 2.0 * np.pi * np.arange(n1)[:, None] * np.arange(n2)[None, :] / n
    return (jnp.asarray(w_fwd, BF16), jnp.asarray(w_inv, BF16),
            jnp.asarray(np.cos(a2), F32), jnp.asarray(-np.sin(a2), F32),
            jnp.asarray(np.cos(at), F32).reshape(n1, 1, n2), jnp.asarray(-np.sin(at), F32).reshape(n1, 1, n2))


def hyena_long_conv(x0, u, ff, fb, bias):
    n, W = u.shape
    n2 = HY_N2
    n1 = 2 * n // n2
    assert 2 * n == n1 * n2 and n1 % (2 * SUBLANES) == 0
    w_fwd, w_inv, w2r, w2i, twr, twi = _dft_tables(n1, n2)
    cols = n2 * W
    assert cols % HY_COLS == 0
    view = lambda a: a.reshape(n1 // 2, cols)
    in_spec = pl.BlockSpec((n1 // 2, HY_COLS), lambda j: (0, j))
    sp_spec = pl.BlockSpec((2 * n1, HY_COLS), lambda j: (0, j))
    spec_shape = jax.ShapeDtypeStruct((2 * n1, cols), BF16)
    au, af, ab = pl.pallas_call(
        _hy_outer_fwd_kernel,
        grid=(cols // HY_COLS,),
        in_specs=[pl.BlockSpec((2 * n1, n1 // 2), lambda j: (0, 0)), in_spec, in_spec, in_spec],
        out_specs=[sp_spec] * 3,
        out_shape=[spec_shape] * 3,
        compiler_params=_params("arbitrary"),
    )(w_fwd, view(u), view(ff), view(fb))
    k_spec = pl.BlockSpec((2, None, n2, W), lambda k: (0, k, 0, 0))
    tw_spec = pl.BlockSpec((None, 1, n2), lambda k: (k, 0, 0))
    w2_spec = pl.BlockSpec((n2, n2), lambda k: (0, 0))
    as4 = lambda a: a.reshape(2, n1, n2, W)
    c = pl.pallas_call(
        _hy_mid_kernel,
        grid=(n1,),
        in_specs=[w2_spec, w2_spec, tw_spec, tw_spec, k_spec, k_spec, k_spec],
        out_specs=k_spec,
        out_shape=jax.ShapeDtypeStruct((2, n1, n2, W), BF16),
        compiler_params=_params("arbitrary"),
    )(w2r, w2i, twr, twi, as4(au), as4(af), as4(ab))
    bias_t = jnp.tile(bias.astype(F32), HY_COLS // W)[None, :]
    out = pl.pallas_call(
        _hy_outer_inv_kernel,
        grid=(cols // HY_COLS,),
        in_specs=[pl.BlockSpec((n1 // 2, 2 * n1), lambda j: (0, 0)), sp_spec, in_spec, in_spec,
                  pl.BlockSpec((1, HY_COLS), lambda j: (0, 0))],
        out_specs=in_spec,
        out_shape=jax.ShapeDtypeStruct((n1 // 2, cols), BF16),
        compiler_params=_params("arbitrary"),
    )(w_inv, c.reshape(2 * n1, cols), view(x0), view(u), bias_t)
    return out.reshape(n, W)


def _hy_small_kernel(f_ref, finv_ref, x0_ref, u_ref, ff_ref, fb_ref, bias_ref, o_ref):
    f = f_ref[...]
    nf = f.shape[0] // 2
    u = u_ref[...]
    spec = lambda a: jnp.dot(f, a, preferred_element_type=F32)
    su, sf, sb = spec(u), spec(ff_ref[...]), spec(fb_ref[...])
    ur, ui = su[:nf], su[nf:]
    hr, hi = sf[:nf] + sb[:nf], sf[nf:] - sb[nf:]
    p = jnp.concatenate([ur * hr - ui * hi, ur * hi + ui * hr], axis=0).astype(BF16)
    conv = jnp.dot(finv_ref[...], p, preferred_element_type=F32)
    o_ref[...] = (x0_ref[...].astype(F32) * (conv + u.astype(F32) * bias_ref[...])).astype(o_ref.dtype)


def hyena_short_conv(x0, u, ff, fb, bias):
    n, W = u.shape
    ang = 2.0 * np.pi * np.arange(2 * n)[:, None] * np.arange(n)[None, :] / (2 * n)
    f = jnp.asarray(np.concatenate([np.cos(ang), -np.sin(ang)], axis=0), BF16)
    finv = jnp.asarray(np.concatenate([np.cos(ang).T, -np.sin(ang).T], axis=1) / (2 * n), BF16)
    args = [f, finv, x0, u, ff, fb, bias.astype(F32)[None, :]]
    return pl.pallas_call(
        _hy_small_kernel,
        grid=(1,),
        in_specs=[pl.BlockSpec(a.shape, lambda i: (0, 0)) for a in args],
        out_specs=pl.BlockSpec((n, W), lambda i: (0, 0)),
        out_shape=jax.ShapeDtypeStruct((n, W), BF16),
        compiler_params=_params("arbitrary"),
    )(*args)


def hyena(z, short_w, short_b, filt, bias):
    n = z.shape[0]
    x0, u = hyena_pre(z, short_w, short_b)
    ff, fb = hyena_filter(n, *filt)
    if 2 * n >= HY_MIN_N1 * HY_N2:
        return hyena_long_conv(x0, u, ff, fb, bias)
    return hyena_short_conv(x0, u, ff, fb, bias)


def _row_tile(m):
    return 1024 if m % 1024 == 0 else m


def _channel_mixer(h, combine, w_gate, w_up, w_down, j, n_experts):
    m = h.shape[0]
    act = expert_glu(h, w_gate, w_up, combine, j * n_experts, n_experts, tm=_row_tile(m), tn=256)
    k = w_down.shape[1]
    if n_experts > 1:
        return matmul_kslabs(act, w_down, j, k // n_experts, tm=_row_tile(m), tn=1024)
    return matmul([act], w_down, j, tm=_row_tile(m), tn=512)


def kernel(x, c, ctx, c_ctx, ada_w, ada_b, norm_g, w_in, w_out, ret_decay, s5_a_re, s5_a_im, s5_log_dt, s5_b_re, s5_b_im, s5_c_re, s5_c_im, s5_d, s5_w_glu, hy_short_w, hy_short_b, hy_w1, hy_b1, hy_w2, hy_b2, hy_w3, hy_b3, hy_freq, hy_bias, ffn_w_gate, ffn_w_up, ffn_w_down, moe_w_router, moe_b_router, moe_w_gate, moe_w_up, moe_w_down):
    depth = ada_w.shape[0]
    xs = {0: x[0], 1: ctx[0]}
    n_lat, n_ctx = xs[0].shape[0], xs[1].shape[0]
    cond = jnp.concatenate([jax.nn.silu(c), jax.nn.silu(c_ctx)[None], jnp.zeros((SUBLANES - 2, D_MODEL), F32)], axis=0)
    mod = ada_modulation(cond.astype(BF16), ada_w, ada_b[:, None, :])
    mod = mod.reshape(depth, SUBLANES, N_MOD, D_MODEL).transpose(0, 2, 1, 3)
    rope = {0: rope_tables(n_lat),
            1: (jnp.ones((n_ctx, RET_DK), F32), jnp.zeros((n_ctx, RET_DK), F32))}
    ones = {r: jnp.ones((xs[r].shape[0], 1), F32) for r in xs}
    moe_gate = moe_w_gate.reshape((-1,) + moe_w_gate.shape[2:])
    moe_up = moe_w_up.reshape((-1,) + moe_w_up.shape[2:])
    moe_down = moe_w_down.reshape(moe_w_down.shape[0], -1, moe_w_down.shape[3])
    hs = {}
    for l in range(depth):
        last = l == depth - 1
        g4, mod_l = norm_g[l], mod[l]
        log_gamma = jax.nn.log_sigmoid(ret_decay[l].astype(F32))
        tabs = s5_tables(s5_a_re[l], s5_a_im[l], s5_log_dt[l], s5_b_re[l], s5_b_im[l], s5_c_re[l], s5_c_im[l])
        filt = (hy_w1[l], hy_b1[l], hy_w2[l], hy_b2[l], hy_w3[l], hy_b3[l], hy_freq[l])
        j = l // 2
        if l % 2 == 1:
            router = (jnp.pad(moe_w_router[j].astype(F32), ((0, 0), (0, ROUTER_PAD - N_EXPERTS))),
                      jnp.concatenate([moe_b_router[j].astype(F32), jnp.full((ROUTER_PAD - N_EXPERTS,), NEG_BIG, F32)])[None, :])
        else:
            router = None
        ret_state = jnp.zeros((2, RET_HEADS, RET_DK, RET_DV), F32)
        s5_state = jnp.zeros((2, SUBLANES, S5_LANES), F32)
        for r in (1, 0):
            xr = xs[r]
            m = xr.shape[0]
            h = hs[r] if l > 0 else prenorm(xr, g4, mod_l, r)
            z = matmul([h], w_in, l, tm=_row_tile(m), tn=512)
            y_ret, ret_fin = retention(z, log_gamma, rope[r][0], rope[r][1], ret_state)
            y_s5, s5_fin = s5(z, tabs, s5_state, s5_d[l][None, :], s5_w_glu, l)
            if r == 1:
                ret_state, s5_state = ret_fin, s5_fin
                if last:
                    continue
            y_hy = hyena(z, hy_short_w[l], hy_short_b[l], filt, hy_bias[l])
            y = matmul([y_ret, y_s5, y_hy], w_out, l, tm=_row_tile(m), tn=512)
            if router is None:
                xr, h = resid_update(xr, y, g4, mod_l, r, 0, has_next=True)
                f = _channel_mixer(h, ones[r], ffn_w_gate, ffn_w_up, ffn_w_down, j, 1)
            else:
                xr, h, combine = resid_update(xr, y, g4, mod_l, r, 0, has_next=True, router=router)
                f = _channel_mixer(h, combine, moe_gate, moe_up, moe_down, j, N_EXPERTS)
            if last:
                xs[r] = resid_update(xr, f, g4, mod_l, r, 1, has_next=False)[0]
            else:
                xs[r], hs[r] = resid_update(xr, f, g4, mod_l, r, 1, has_next=True, nxt=(norm_g[l + 1], mod[l + 1]))
    return xs[0][None]
```
